```python
import math
import jax, jax.numpy as jnp
from jax import lax
import numpy as np

D_MODEL = 4096
BATCH = 4
SEQ = 2048
DEPTH = 4

N_MIXERS = 2
N_A_LAYERS = (DEPTH + N_MIXERS - 1) // N_MIXERS
N_B_LAYERS = DEPTH // N_MIXERS

ROPE_THETA = 10000.0
NORM_EPS = 1e-6
COND_RANK = 512
N_MOD = 6

MLA_HEADS = 32
MLA_Q_RANK = 1536
MLA_KV_RANK = 512
MLA_NOPE_DIM = 128
MLA_ROPE_DIM = 64
MLA_V_DIM = 128
MLA_IN_DIM = MLA_Q_RANK + MLA_KV_RANK + MLA_ROPE_DIM
Q_BLOCK = 128

DIL_GROUPS = ((128, 1), (512, 4), (2048, 16))
N_DIL_GROUPS = len(DIL_GROUPS)
DIL_HEADS = 16
DIL_HEAD_DIM = 128
DIL_QKV_DIM = 3 * N_DIL_GROUPS * DIL_HEADS * DIL_HEAD_DIM
DIL_OUT_DIM = DIL_HEADS * DIL_HEAD_DIM

FFN_HIDDEN = -(-8 * D_MODEL // (3 * 256)) * 256

kernel_name = 'hybrid_mla_dilated_adaln_trunk'


def rms_norm(x, g):
    xf = x.astype(jnp.float32)
    y = xf * lax.rsqrt(jnp.mean(xf * xf, axis=-1, keepdims=True) + NORM_EPS)
    return (y * g.astype(jnp.float32)).astype(x.dtype)


def rope(x, pos):
    d = x.shape[-1]
    half = d // 2
    inv_freq = jnp.exp(jnp.arange(half, dtype=jnp.float32) * (-2.0 * math.log(ROPE_THETA) / d))
    ang = pos.astype(jnp.float32)[:, :, None] * inv_freq
    cos = jnp.cos(ang)[:, :, None, :]
    sin = jnp.sin(ang)[:, :, None, :]
    xf = x.astype(jnp.float32)
    x1, x2 = xf[..., :half], xf[..., half:]
    return jnp.concatenate([x1 * cos - x2 * sin, x2 * cos + x1 * sin], axis=-1).astype(x.dtype)


def mla_mixer(h, pos, w_in, g_q_a, g_kv_a, w_q_b, w_kv_b, g_q_nope, g_q_pe, g_k_nope, g_k_pe, w_o):
    B, S, _ = h.shape
    a = h @ w_in
    cq, ckv, k_pe = jnp.split(a, [MLA_Q_RANK, MLA_Q_RANK + MLA_KV_RANK], axis=-1)
    q = (rms_norm(cq, g_q_a) @ w_q_b).reshape(B, S, MLA_HEADS, MLA_NOPE_DIM + MLA_ROPE_DIM)
    kv = (rms_norm(ckv, g_kv_a) @ w_kv_b).reshape(B, S, MLA_HEADS, MLA_NOPE_DIM + MLA_V_DIM)
    q_nope = rms_norm(q[..., :MLA_NOPE_DIM], g_q_nope)
    q_pe = rope(rms_norm(q[..., MLA_NOPE_DIM:], g_q_pe), pos)
    k_nope = rms_norm(kv[..., :MLA_NOPE_DIM], g_k_nope)
    v = kv[..., MLA_NOPE_DIM:]
    k_pe = rope(rms_norm(k_pe, g_k_pe)[:, :, None, :], pos)[:, :, 0, :]
    scale = (MLA_NOPE_DIM + MLA_ROPE_DIM) ** -0.5
    nb = S // Q_BLOCK

    def to_blocks(t):
        return jnp.moveaxis(t.reshape(B, nb, Q_BLOCK, *t.shape[2:]), 1, 0)

    key_idx = jnp.arange(S, dtype=jnp.int32)

    def attend_block(args):
        qn, qp, start = args
        s = (jnp.einsum('bqhd,bkhd->bhqk', qn, k_nope)
             + jnp.einsum('bqhr,bkr->bhqk', qp, k_pe)).astype(jnp.float32) * scale
        q_idx = start + jnp.arange(Q_BLOCK, dtype=jnp.int32)
        s = jnp.where(key_idx[None, :] <= q_idx[:, None], s, -jnp.inf)
        p = jax.nn.softmax(s, axis=-1)
        return jnp.einsum('bhqk,bkhd->bqhd', p.astype(v.dtype), v)

    starts = jnp.arange(nb, dtype=jnp.int32) * Q_BLOCK
    o = lax.map(attend_block, (to_blocks(q_nope), to_blocks(q_pe), starts))
    o = jnp.moveaxis(o, 0, 1).reshape(B, S, MLA_HEADS * MLA_V_DIM)
    return o @ w_o


def dilated_group_attention(q, k, v, window, dilation):
    B, S, H, d = q.shape
    span = window // dilation
    blk = span
    L = S // dilation
    nb = -(-L // blk)
    Lp = nb * blk

    def to_sub(t):
        t = t.reshape(B, L, dilation, H, d).transpose(0, 2, 1, 3, 4)
        t = jnp.pad(t, ((0, 0), (0, 0), (0, Lp - L), (0, 0), (0, 0)))
        return t.reshape(B, dilation, nb, blk, H, d)

    def band(t):
        prev = jnp.pad(t, ((0, 0), (0, 0), (1, 0), (0, 0), (0, 0), (0, 0)))[:, :, :-1]
        return jnp.concatenate([prev, t], axis=3)

    qb = to_sub(q)
    kb = band(to_sub(k))
    vb = band(to_sub(v))
    s = jnp.einsum('brnqhd,brnkhd->brnhqk', qb, kb).astype(jnp.float32) * (d ** -0.5)
    qi = jnp.arange(blk)[:, None]
    kj = jnp.arange(2 * blk)[None, :]
    dist = blk + qi - kj
    first = (jnp.arange(nb) == 0)[:, None, None]
    valid = (dist >= 0) & (dist <= span) & ~(first & (kj < blk))
    s = jnp.where(valid[None, None, :, None], s, -jnp.inf)
    m = jnp.max(s, axis=-1, keepdims=True)
    e = jnp.exp(s - m)
    den = jnp.sum(e, axis=-1, keepdims=True)
    lse = (m + jnp.log(den))[..., 0]
    o = jnp.einsum('brnhqk,brnkhd->brnqhd', (e / den).astype(v.dtype), vb)

    def from_sub(t):
        t = t.reshape(B, dilation, Lp, *t.shape[4:])[:, :, :L]
        return jnp.swapaxes(t, 1, 2).reshape(B, S, *t.shape[3:])

    return from_sub(o), from_sub(jnp.swapaxes(lse, 3, 4))


def dilated_mixer(h, pos, w_qkv, g_q, g_k, w_o):
    B, S, _ = h.shape
    G, HG, dh = N_DIL_GROUPS, DIL_HEADS, DIL_HEAD_DIM
    qkv = (h @ w_qkv).reshape(B, S, 3, G, HG, dh)
    q = rope(rms_norm(qkv[:, :, 0], g_q[:, None, :]).reshape(B, S, G * HG, dh), pos)
    k = rope(rms_norm(qkv[:, :, 1], g_k[:, None, :]).reshape(B, S, G * HG, dh), pos)
    q = q.reshape(B, S, G, HG, dh)
    k = k.reshape(B, S, G, HG, dh)
    v = qkv[:, :, 2]
    outs, lses = [], []
    for gi, (window, dilation) in enumerate(DIL_GROUPS):
        o_g, l_g = dilated_group_attention(q[:, :, gi], k[:, :, gi], v[:, :, gi], window, dilation)
        outs.append(o_g)
        lses.append(l_g)
    alpha = jax.nn.softmax(jnp.stack(lses).astype(jnp.float32), axis=0)
    o = jnp.sum(alpha[..., None] * jnp.stack(outs).astype(jnp.float32), axis=0).astype(h.dtype)
    return o.reshape(B, S, DIL_OUT_DIM) @ w_o


def swiglu(h, w_gate, w_up, w_down):
    return (jax.nn.silu(h @ w_gate) * (h @ w_up)) @ w_down


def setup_inputs(seed: int = 0) -> dict:
    key = jax.random.key(seed)
    ks = jax.random.split(key, 26)
    D = D_MODEL

    def nrm(k, shape, scale):
        return jax.random.normal(k, shape, jnp.float32) * scale

    def gain(k, shape):
        return 1.0 + 0.02 * jax.random.normal(k, shape, jnp.float32)

    offset = jax.random.randint(ks[2], (BATCH,), 0, 1024, dtype=jnp.int32)
    positions = (offset[:, None] + jnp.arange(SEQ, dtype=jnp.int32)[None, :]).astype(jnp.int32)
    qk_dim = MLA_NOPE_DIM + MLA_ROPE_DIM
    return {
        'x': nrm(ks[0], (BATCH, SEQ, D), 1.0),
        'c': nrm(ks[1], (BATCH, D), 1.0),
        'positions': positions,
        'w_cond': nrm(ks[3], (D, COND_RANK), D ** -0.5),
        'b_cond': nrm(ks[4], (COND_RANK,), 0.02),
        'w_mod': nrm(ks[5], (DEPTH, COND_RANK, N_MOD * D), 0.5 * COND_RANK ** -0.5),
        'b_mod': nrm(ks[6], (DEPTH, N_MOD * D), 0.02),
        'g_mix_norm': gain(ks[7], (DEPTH, D)),
        'g_ffn_norm': gain(ks[8], (DEPTH, D)),
        'mla_w_in': nrm(ks[9], (N_A_LAYERS, D, MLA_IN_DIM), D ** -0.5),
        'mla_g_q_a': gain(ks[10], (N_A_LAYERS, MLA_Q_RANK)),
        'mla_g_kv_a': gain(ks[11], (N_A_LAYERS, MLA_KV_RANK)),
        'mla_w_q_b': nrm(ks[12], (N_A_LAYERS, MLA_Q_RANK, MLA_HEADS * qk_dim), MLA_Q_RANK ** -0.5),
        'mla_w_kv_b': nrm(ks[13], (N_A_LAYERS, MLA_KV_RANK, MLA_HEADS * (MLA_NOPE_DIM + MLA_V_DIM)), MLA_KV_RANK ** -0.5),
        'mla_g_q_nope': gain(ks[14], (N_A_LAYERS, MLA_NOPE_DIM)),
        'mla_g_q_pe': gain(ks[15], (N_A_LAYERS, MLA_ROPE_DIM)),
        'mla_g_k_nope': gain(ks[16], (N_A_LAYERS, MLA_NOPE_DIM)),
        'mla_g_k_pe': gain(ks[17], (N_A_LAYERS, MLA_ROPE_DIM)),
        'mla_w_o': nrm(ks[18], (N_A_LAYERS, MLA_HEADS * MLA_V_DIM, D), (MLA_HEADS * MLA_V_DIM) ** -0.5),
        'dil_w_qkv': nrm(ks[19], (N_B_LAYERS, D, DIL_QKV_DIM), D ** -0.5),
        'dil_g_q': gain(ks[20], (N_B_LAYERS, N_DIL_GROUPS, DIL_HEAD_DIM)),
        'dil_g_k': gain(ks[21], (N_B_LAYERS, N_DIL_GROUPS, DIL_HEAD_DIM)),
        'dil_w_o': nrm(ks[22], (N_B_LAYERS, DIL_OUT_DIM, D), DIL_OUT_DIM ** -0.5),
        'ffn_w_gate': nrm(ks[23], (DEPTH, D, FFN_HIDDEN), D ** -0.5),
        'ffn_w_up': nrm(ks[24], (DEPTH, D, FFN_HIDDEN), D ** -0.5),
        'ffn_w_down': nrm(ks[25], (DEPTH, FFN_HIDDEN, D), FFN_HIDDEN ** -0.5),
    }


def reference(x, c, positions, w_cond, b_cond, w_mod, b_mod, g_mix_norm, g_ffn_norm,
              mla_w_in, mla_g_q_a, mla_g_kv_a, mla_w_q_b, mla_w_kv_b, mla_g_q_nope,
              mla_g_q_pe, mla_g_k_nope, mla_g_k_pe, mla_w_o, dil_w_qkv, dil_g_q, dil_g_k,
              dil_w_o, ffn_w_gate, ffn_w_up, ffn_w_down):
    e = jax.nn.silu(c @ w_cond + b_cond)
    for i in range(DEPTH):
        mod = (e @ w_mod[i] + b_mod[i])[:, None, :]
        sh_m, sc_m, gt_m, sh_f, sc_f, gt_f = jnp.split(mod, N_MOD, axis=-1)
        h = rms_norm(x, g_mix_norm[i]) * (1.0 + sc_m) + sh_m
        j = i // N_MIXERS
        if i % N_MIXERS == 0:
            y = mla_mixer(h, positions, mla_w_in[j], mla_g_q_a[j], mla_g_kv_a[j], mla_w_q_b[j],
                          mla_w_kv_b[j], mla_g_q_nope[j], mla_g_q_pe[j], mla_g_k_nope[j],
                          mla_g_k_pe[j], mla_w_o[j])
        else:
            y = dilated_mixer(h, positions, dil_w_qkv[j], dil_g_q[j], dil_g_k[j], dil_w_o[j])
        x = x + gt_m * y
        h = rms_norm(x, g_ffn_norm[i]) * (1.0 + sc_f) + sh_f
        x = x + gt_f * swiglu(h, ffn_w_gate[i], ffn_w_up[i], ffn_w_down[i])
    return x
```

```python
import functools
import math

import jax
import jax.numpy as jnp
from jax import lax
from jax.experimental import pallas as pl
from jax.experimental.pallas import tpu as pltpu

F32 = jnp.float32
BF16 = jnp.bfloat16

ROPE_THETA = 10000.0
NORM_EPS = 1e-6
N_MOD = 6

MLA_HEADS = 32
MLA_Q_RANK = 1536
MLA_KV_RANK = 512
MLA_NOPE_DIM = 128
MLA_ROPE_DIM = 64
MLA_V_DIM = 128

DIL_GROUPS = ((128, 1), (512, 4), (2048, 16))
DIL_HEADS = 16
DIL_HEAD_DIM = 128

LANES = 128
V7X_VMEM_LIMIT_BYTES = 56 * 2**20

FFN_PAD_MULTIPLE = 1024
MLA_IN_PAD = 2304


def _params(*semantics):
    return pltpu.CompilerParams(dimension_semantics=semantics,
                                vmem_limit_bytes=V7X_VMEM_LIMIT_BYTES)


def _dot(a, b):
    return jnp.dot(a, b, preferred_element_type=F32)


def _dot_nt(a, b):
    return lax.dot_general(a, b, (((1,), (1,)), ((), ())), preferred_element_type=F32)


def _rms(x, gain, inv_n=None):
    if inv_n is None:
        ms = jnp.mean(x * x, axis=-1, keepdims=True)
    else:
        ms = jnp.sum(x * x, axis=-1, keepdims=True) * inv_n
    return x * lax.rsqrt(ms + NORM_EPS) * gain


def _cond_mod_kernel(c_ref, wc_ref, bc_ref, wm_ref, bm_ref, o_ref, e_ref):
    @pl.when((pl.program_id(0) == 0) & (pl.program_id(1) == 0))
    def _():
        z = _dot(c_ref[...].astype(BF16), wc_ref[...].astype(BF16)) + bc_ref[...]
        e_ref[...] = z * jax.nn.sigmoid(z)

    o_ref[...] = _dot(e_ref[...].astype(BF16), wm_ref[...].astype(BF16)) + bm_ref[...]


def _cond_mod(c, w_cond, b_cond, w_mod, b_mod):
    batch, d = c.shape
    depth, rank, n = w_mod.shape
    rows = 8
    tn = 2048
    c8 = jnp.pad(c, ((0, rows - batch), (0, 0)))
    out = pl.pallas_call(
        _cond_mod_kernel,
        grid=(depth, n // tn),
        in_specs=[
            pl.BlockSpec((rows, d), lambda l, j: (0, 0)),
            pl.BlockSpec((d, rank), lambda l, j: (0, 0)),
            pl.BlockSpec((1, rank), lambda l, j: (0, 0)),
            pl.BlockSpec((None, rank, tn), lambda l, j: (l, 0, j)),
            pl.BlockSpec((None, 1, tn), lambda l, j: (l, 0, j)),
        ],
        out_specs=pl.BlockSpec((None, rows, tn), lambda l, j: (l, 0, j)),
        out_shape=jax.ShapeDtypeStruct((depth, rows, n), F32),
        scratch_shapes=[pltpu.VMEM((rows, rank), F32)],
        compiler_params=_params("arbitrary", "arbitrary"),
        name="cond_mod",
    )(c8, w_cond, b_cond.reshape(1, rank), w_mod, b_mod.reshape(depth, 1, n))
    return out[:, :batch].reshape(depth * batch, 1, n)


def _rope_tables_kernel(pos_ref, cm_ref, s1_ref, s2_ref, cd_ref, sd_ref):
    pos = pos_ref[...].astype(F32)
    lane = lax.broadcasted_iota(jnp.int32, (1, LANES), 1)

    def angles(dim):
        half = dim // 2
        idx = (lane & (half - 1)).astype(F32)
        inv_freq = jnp.exp(idx * (-2.0 * math.log(ROPE_THETA) / dim))
        return pos * inv_freq

    ang = angles(MLA_ROPE_DIM)
    cos, sin = jnp.cos(ang), jnp.sin(ang)
    half = MLA_ROPE_DIM // 2
    cm_ref[...] = jnp.where(lane < MLA_ROPE_DIM, cos, 0.0)
    s1_ref[...] = jnp.where(lane < half, -sin, 0.0)
    s2_ref[...] = jnp.where((lane >= half) & (lane < MLA_ROPE_DIM), sin, 0.0)
    ang = angles(DIL_HEAD_DIM)
    cos, sin = jnp.cos(ang), jnp.sin(ang)
    cd_ref[...] = cos
    sd_ref[...] = jnp.where(lane < DIL_HEAD_DIM // 2, -sin, sin)


def _rope_tables(positions):
    batch, seq = positions.shape
    ts = 512
    m = batch * seq
    table = jax.ShapeDtypeStruct((m, LANES), F32)
    spec = pl.BlockSpec((ts, LANES), lambda i: (i, 0))
    return pl.pallas_call(
        _rope_tables_kernel,
        grid=(m // ts,),
        in_specs=[pl.BlockSpec((ts, 1), lambda i: (i, 0))],
        out_specs=[spec] * 5,
        out_shape=[table] * 5,
        compiler_params=_params("parallel"),
        name="rope_tables",
    )(positions.reshape(m, 1))


def _norm_mod_kernel(x_ref, g_ref, sc_ref, sh_ref, o_ref):
    y = _rms(x_ref[...], g_ref[...])
    o_ref[...] = (y * (1.0 + sc_ref[...]) + sh_ref[...]).astype(o_ref.dtype)


def _norm_mod(x, gain, mod, layer, shift_idx, scale_idx, batch, seq):
    m, d = x.shape
    tm = 512
    per_batch = seq // tm

    def mod_spec(which):
        return pl.BlockSpec((None, 1, d), lambda i: (layer * batch + i // per_batch, 0, which))

    return pl.pallas_call(
        _norm_mod_kernel,
        grid=(m // tm,),
        in_specs=[
            pl.BlockSpec((tm, d), lambda i: (i, 0)),
            pl.BlockSpec((None, 1, d), lambda i: (layer, 0, 0)),
            mod_spec(scale_idx),
            mod_spec(shift_idx),
        ],
        out_specs=pl.BlockSpec((tm, d), lambda i: (i, 0)),
        out_shape=jax.ShapeDtypeStruct((m, d), BF16),
        compiler_params=_params("parallel"),
        name="norm_mod",
    )(x, gain, mod, mod)


def _mm_kernel(x_ref, w_ref, o_ref):
    o_ref[...] = _dot(x_ref[...], w_ref[...]).astype(o_ref.dtype)


def _matmul(x, w, layer, *, tm, tn, out_dtype):
    m, k = x.shape
    n = w.shape[-1]
    return pl.pallas_call(
        _mm_kernel,
        grid=(m // tm, n // tn),
        in_specs=[
            pl.BlockSpec((tm, k), lambda i, j: (i, 0)),
            pl.BlockSpec((None, k, tn), lambda i, j: (layer, 0, j)),
        ],
        out_specs=pl.BlockSpec((tm, tn), lambda i, j: (i, j)),
        out_shape=jax.ShapeDtypeStruct((m, n), out_dtype),
        compiler_params=_params("parallel", "arbitrary"),
        name="matmul",
    )(x, w)


def _mm_resid_kernel(x_ref, w_ref, r_ref, gt_ref, o_ref):
    o_ref[...] = r_ref[...] + gt_ref[...] * _dot(x_ref[...], w_ref[...])


def _mm_resid_ksplit_kernel(x_ref, w_ref, r_ref, gt_ref, o_ref, acc_ref):
    kk = pl.program_id(2)

    @pl.when(kk == 0)
    def _():
        acc_ref[...] = jnp.zeros_like(acc_ref)

    acc_ref[...] += _dot(x_ref[...], w_ref[...])

    @pl.when(kk == pl.num_programs(2) - 1)
    def _():
        o_ref[...] = r_ref[...] + gt_ref[...] * acc_ref[...]


def _matmul_resid(x, w, layer, resid, mod, mod_layer, gate_idx, batch, seq, *, tm, tn, tk=None):
    m, k = x.shape
    n = w.shape[-1]
    per_batch = seq // tm
    gate_blocks = n // tn
    if tk is None:
        return pl.pallas_call(
            _mm_resid_kernel,
            grid=(m // tm, n // tn),
            in_specs=[
                pl.BlockSpec((tm, k), lambda i, j: (i, 0)),
                pl.BlockSpec((None, k, tn), lambda i, j: (layer, 0, j)),
                pl.BlockSpec((tm, tn), lambda i, j: (i, j)),
                pl.BlockSpec((None, 1, tn), lambda i, j: (mod_layer * batch + i // per_batch, 0,
                                                          gate_idx * gate_blocks + j)),
            ],
            out_specs=pl.BlockSpec((tm, tn), lambda i, j: (i, j)),
            out_shape=jax.ShapeDtypeStruct((m, n), F32),
            compiler_params=_params("parallel", "arbitrary"),
            name="matmul_resid",
        )(x, w, resid, mod)
    return pl.pallas_call(
        _mm_resid_ksplit_kernel,
        grid=(m // tm, n // tn, k // tk),
        in_specs=[
            pl.BlockSpec((tm, tk), lambda i, j, kk: (i, kk)),
            pl.BlockSpec((None, tk, tn), lambda i, j, kk: (layer, kk, j)),
            pl.BlockSpec((tm, tn), lambda i, j, kk: (i, j)),
            pl.BlockSpec((None, 1, tn), lambda i, j, kk: (mod_layer * batch + i // per_batch, 0,
                                                          gate_idx * gate_blocks + j)),
        ],
        out_specs=pl.BlockSpec((tm, tn), lambda i, j, kk: (i, j)),
        out_shape=jax.ShapeDtypeStruct((m, n), F32),
        scratch_shapes=[pltpu.VMEM((tm, tn), F32)],
        compiler_params=_params("parallel", "parallel", "arbitrary"),
        name="matmul_resid_ksplit",
    )(x, w, resid, mod)


def _swiglu_kernel(x_ref, wg_ref, wu_ref, o_ref):
    x = x_ref[...]
    g = _dot(x, wg_ref[...])
    u = _dot(x, wu_ref[...])
    o_ref[...] = (g * jax.nn.sigmoid(g) * u).astype(o_ref.dtype)


def _swiglu_up(x, wg, wu, layer, *, tm, tn):
    m, k = x.shape
    n = wg.shape[-1]
    w_spec = pl.BlockSpec((None, k, tn), lambda i, j: (layer, 0, j))
    return pl.pallas_call(
        _swiglu_kernel,
        grid=(m // tm, n // tn),
        in_specs=[pl.BlockSpec((tm, k), lambda i, j: (i, 0)), w_spec, w_spec],
        out_specs=pl.BlockSpec((tm, tn), lambda i, j: (i, j)),
        out_shape=jax.ShapeDtypeStruct((m, n), BF16),
        compiler_params=_params("parallel", "arbitrary"),
        name="swiglu_up",
    )(x, wg, wu)


def _rope_mla(x, cos, s1, s2):
    half = MLA_ROPE_DIM // 2
    return (x * cos + pltpu.roll(x, LANES - half, 1) * s1 + pltpu.roll(x, half, 1) * s2)


def _rope_dil(x, cos, sin_signed):
    return x * cos + pltpu.roll(x, DIL_HEAD_DIM // 2, 1) * sin_signed


MLA_Q_HEAD_PAD = 2 * LANES


def _mla_q_kernel(cq_ref, gqa_ref, w_ref, gn_ref, gp_ref, cos_ref, s1_ref, s2_ref, o_ref, hs_ref):
    @pl.when(pl.program_id(1) == 0)
    def _():
        hs_ref[...] = _rms(cq_ref[...], gqa_ref[...]).astype(hs_ref.dtype)

    acc = _dot(hs_ref[...], w_ref[...])
    cos, s1, s2 = cos_ref[...], s1_ref[...], s2_ref[...]
    for hh in range(acc.shape[1] // MLA_Q_HEAD_PAD):
        c0 = hh * MLA_Q_HEAD_PAD
        nope = _rms(acc[:, c0:c0 + LANES], gn_ref[...])
        pe = _rms(acc[:, c0 + LANES:c0 + 2 * LANES], gp_ref[...], inv_n=1.0 / MLA_ROPE_DIM)
        o_ref[:, c0:c0 + LANES] = nope.astype(o_ref.dtype)
        o_ref[:, c0 + LANES:c0 + 2 * LANES] = _rope_mla(pe, cos, s1, s2).astype(o_ref.dtype)


def _mla_q_proj(a, g_q_a, w, layer, g_nope, g_pe_pad, tables, *, tm, tn):
    m = a.shape[0]
    k, n = w.shape[1:]
    cos, s1, s2 = tables
    tab = pl.BlockSpec((tm, LANES), lambda i, j: (i, 0))
    return pl.pallas_call(
        _mla_q_kernel,
        grid=(m // tm, n // tn),
        in_specs=[
            pl.BlockSpec((tm, k), lambda i, j: (i, 0)),
            pl.BlockSpec((None, 1, k), lambda i, j: (layer, 0, 0)),
            pl.BlockSpec((None, k, tn), lambda i, j: (layer, 0, j)),
            pl.BlockSpec((None, 1, LANES), lambda i, j: (layer, 0, 0)),
            pl.BlockSpec((None, 1, LANES), lambda i, j: (layer, 0, 0)),
            tab, tab, tab,
        ],
        out_specs=pl.BlockSpec((tm, tn), lambda i, j: (i, j)),
        out_shape=jax.ShapeDtypeStruct((m, n), BF16),
        scratch_shapes=[pltpu.VMEM((tm, k), BF16)],
        compiler_params=_params("parallel", "arbitrary"),
        name="mla_q_proj",
    )(a, g_q_a, w, g_nope, g_pe_pad, cos, s1, s2)


MLA_KV_BLOCK = 768


def _mla_kv_kernel(ckv_ref, gkva_ref, w_ref, gk_ref, gp_ref, cos_ref, s1_ref, s2_ref,
                   kv_ref, kpe_ref, hs_ref):
    @pl.when(pl.program_id(1) == 0)
    def _():
        blk = ckv_ref[...]
        hs_ref[...] = _rms(blk[:, :MLA_KV_RANK], gkva_ref[...]).astype(hs_ref.dtype)
        pe = _rms(blk[:, MLA_KV_RANK:MLA_KV_RANK + LANES], gp_ref[...], inv_n=1.0 / MLA_ROPE_DIM)
        kpe_ref[...] = _rope_mla(pe, cos_ref[...], s1_ref[...], s2_ref[...]).astype(kpe_ref.dtype)

    acc = _dot(hs_ref[...], w_ref[...])
    head_w = MLA_NOPE_DIM + MLA_V_DIM
    for hh in range(acc.shape[1] // head_w):
        c0 = hh * head_w
        kv_ref[:, c0:c0 + MLA_NOPE_DIM] = _rms(acc[:, c0:c0 + MLA_NOPE_DIM], gk_ref[...]).astype(kv_ref.dtype)
        kv_ref[:, c0 + MLA_NOPE_DIM:c0 + head_w] = acc[:, c0 + MLA_NOPE_DIM:c0 + head_w].astype(kv_ref.dtype)


def _mla_kv_proj(a, g_kv_a, w, layer, g_k_nope, g_pe_pad, tables, *, tm, tn):
    m = a.shape[0]
    k, n = w.shape[1:]
    cos, s1, s2 = tables
    tab = pl.BlockSpec((tm, LANES), lambda i, j: (i, 0))
    kv_col_block = MLA_Q_RANK // MLA_KV_BLOCK
    return pl.pallas_call(
        _mla_kv_kernel,
        grid=(m // tm, n // tn),
        in_specs=[
            pl.BlockSpec((tm, MLA_KV_BLOCK), lambda i, j: (i, kv_col_block)),
            pl.BlockSpec((None, 1, k), lambda i, j: (layer, 0, 0)),
            pl.BlockSpec((None, k, tn), lambda i, j: (layer, 0, j)),
            pl.BlockSpec((None, 1, LANES), lambda i, j: (layer, 0, 0)),
            pl.BlockSpec((None, 1, LANES), lambda i, j: (layer, 0, 0)),
            tab, tab, tab,
        ],
        out_specs=[
            pl.BlockSpec((tm, tn), lambda i, j: (i, j)),
            pl.BlockSpec((tm, LANES), lambda i, j: (i, 0)),
        ],
        out_shape=[
            jax.ShapeDtypeStruct((m, n), BF16),
            jax.ShapeDtypeStruct((m, LANES), BF16),
        ],
        scratch_shapes=[pltpu.VMEM((tm, k), BF16)],
        compiler_params=_params("parallel", "arbitrary"),
        name="mla_kv_proj",
    )(a, g_kv_a, w, g_k_nope, g_pe_pad, cos, s1, s2)


def _softmax_step(carry, s, v):
    m_prev, l_prev, acc_prev = carry
    m_new = jnp.maximum(m_prev, jnp.max(s, axis=-1, keepdims=True))
    p = jnp.exp(s - m_new)
    alpha = jnp.exp(m_prev - m_new)
    l_new = alpha * l_prev + jnp.sum(p, axis=-1, keepdims=True)
    acc_new = alpha * acc_prev + _dot(p.astype(v.dtype), v)
    return m_new, l_new, acc_new


def _softmax_init(tq, dv):
    return (jnp.full((tq, 1), -jnp.inf, F32), jnp.zeros((tq, 1), F32), jnp.zeros((tq, dv), F32))


ATTN_BLOCK = 512


def _mla_attn_kernel(q_ref, kn_ref, v_ref, kpe_ref, o_ref, kcat_ref):
    qi = pl.program_id(2)
    tq = q_ref.shape[0]
    tk = tq
    scale = (MLA_NOPE_DIM + MLA_ROPE_DIM) ** -0.5

    @pl.when(qi == 0)
    def _():
        kcat_ref[:, :MLA_NOPE_DIM] = kn_ref[...]
        kcat_ref[:, MLA_NOPE_DIM:] = kpe_ref[...]

    q = q_ref[...]

    def scores(kb):
        start = pl.multiple_of(kb * tk, tk)
        return _dot_nt(q, kcat_ref[pl.ds(start, tk), :]) * scale, v_ref[pl.ds(start, tk), :]

    def full_block(kb, carry):
        s, v = scores(kb)
        return _softmax_step(carry, s, v)

    carry = lax.fori_loop(0, qi, full_block, _softmax_init(tq, MLA_V_DIM))
    s, v = scores(qi)
    row = lax.broadcasted_iota(jnp.int32, (tq, tk), 0)
    col = lax.broadcasted_iota(jnp.int32, (tq, tk), 1)
    s = jnp.where(col <= row, s, -jnp.inf)
    _, l_fin, acc = _softmax_step(carry, s, v)
    o_ref[...] = (acc / l_fin).astype(o_ref.dtype)


def _mla_attention(q, kv, kpe, batch, seq):
    heads = MLA_HEADS
    tq = ATTN_BLOCK
    nq = seq // tq
    return pl.pallas_call(
        _mla_attn_kernel,
        grid=(batch, heads, nq),
        in_specs=[
            pl.BlockSpec((tq, MLA_Q_HEAD_PAD), lambda b, h, qi: (b * nq + qi, h)),
            pl.BlockSpec((seq, MLA_NOPE_DIM), lambda b, h, qi: (b, 2 * h)),
            pl.BlockSpec((seq, MLA_V_DIM), lambda b, h, qi: (b, 2 * h + 1)),
            pl.BlockSpec((seq, LANES), lambda b, h, qi: (b, 0)),
        ],
        out_specs=pl.BlockSpec((tq, MLA_V_DIM), lambda b, h, qi: (b * nq + qi, h)),
        out_shape=jax.ShapeDtypeStruct((batch * seq, heads * MLA_V_DIM), BF16),
        scratch_shapes=[pltpu.VMEM((seq, MLA_NOPE_DIM + LANES), BF16)],
        compiler_params=_params("parallel", "parallel", "arbitrary"),
        name="mla_attention",
    )(q, kv, kv, kpe)


def _dil_qkv_kernel(n_qk_blocks, x_ref, w_ref, gain_ref, cos_ref, sin_ref, o_ref):
    acc = _dot(x_ref[...], w_ref[...])
    j = pl.program_id(1)

    @pl.when(j < n_qk_blocks)
    def _():
        cos, sin = cos_ref[...], sin_ref[...]
        for hh in range(acc.shape[1] // DIL_HEAD_DIM):
            c0 = hh * DIL_HEAD_DIM
            y = _rms(acc[:, c0:c0 + DIL_HEAD_DIM], gain_ref[:, c0:c0 + DIL_HEAD_DIM])
            o_ref[:, c0:c0 + DIL_HEAD_DIM] = _rope_dil(y, cos, sin).astype(o_ref.dtype)

    @pl.when(j >= n_qk_blocks)
    def _():
        o_ref[...] = acc.astype(o_ref.dtype)


def _dil_qkv_proj(x, w, layer, gain_cols, tables, *, tm, tn):
    m, k = x.shape
    n = w.shape[-1]
    cos, sin = tables
    n_qk_blocks = (2 * n // 3) // tn
    tab = pl.BlockSpec((tm, LANES), lambda i, j: (i, 0))
    return pl.pallas_call(
        functools.partial(_dil_qkv_kernel, n_qk_blocks),
        grid=(m // tm, n // tn),
        in_specs=[
            pl.BlockSpec((tm, k), lambda i, j: (i, 0)),
            pl.BlockSpec((None, k, tn), lambda i, j: (layer, 0, j)),
            pl.BlockSpec((None, 1, tn), lambda i, j: (layer, 0, j)),
            tab, tab,
        ],
        out_specs=pl.BlockSpec((tm, tn), lambda i, j: (i, j)),
        out_shape=jax.ShapeDtypeStruct((m, n), BF16),
        compiler_params=_params("parallel", "arbitrary"),
        name="dil_qkv_proj",
    )(x, w, gain_cols, cos, sin)


def _dil_attn_kernel(*refs):
    n_groups = len(DIL_GROUPS)
    q_refs = refs[:n_groups]
    k_refs = refs[n_groups:2 * n_groups]
    v_refs = refs[2 * n_groups:3 * n_groups]
    o_ref = refs[3 * n_groups]
    qi = pl.program_id(2)
    tq = q_refs[0].shape[0]
    tk = tq
    scale = DIL_HEAD_DIM ** -0.5
    t0 = qi * tq
    row = t0 + lax.broadcasted_iota(jnp.int32, (tq, tk), 0)
    col_local = lax.broadcasted_iota(jnp.int32, (tq, tk), 1)

    carry = _softmax_init(tq, DIL_HEAD_DIM)
    for g, (window, dilation) in enumerate(DIL_GROUPS):
        q = q_refs[g][...]

        def block(kb, carry, q=q, g=g, window=window, dilation=dilation):
            start = pl.multiple_of(kb * tk, tk)
            s = _dot_nt(q, k_refs[g][pl.ds(start, tk), :]) * scale
            dist = row - (start + col_local)
            valid = (dist >= 0) & (dist <= window) & ((dist & (dilation - 1)) == 0)
            s = jnp.where(valid, s, -jnp.inf)
            return _softmax_step(carry, s, v_refs[g][pl.ds(start, tk), :])

        carry = block(qi, carry)
        first = jnp.maximum(t0 - window, 0) // tk
        carry = lax.fori_loop(first, qi, block, carry)

    _, l_fin, acc = carry
    o_ref[...] = (acc / l_fin).astype(o_ref.dtype)


def _dil_attention(qkv, batch, seq):
    n_groups = len(DIL_GROUPS)
    tq = ATTN_BLOCK
    nq = seq // tq
    hd = DIL_HEAD_DIM

    def col(part, g):
        return (part * n_groups + g) * DIL_HEADS

    q_specs = [pl.BlockSpec((tq, hd), lambda b, h, qi, c=col(0, g): (b * nq + qi, c + h))
               for g in range(n_groups)]
    k_specs = [pl.BlockSpec((seq, hd), lambda b, h, qi, c=col(1, g): (b, c + h))
               for g in range(n_groups)]
    v_specs = [pl.BlockSpec((seq, hd), lambda b, h, qi, c=col(2, g): (b, c + h))
               for g in range(n_groups)]
    return pl.pallas_call(
        _dil_attn_kernel,
        grid=(batch, DIL_HEADS, nq),
        in_specs=q_specs + k_specs + v_specs,
        out_specs=pl.BlockSpec((tq, hd), lambda b, h, qi: (b * nq + qi, h)),
        out_shape=jax.ShapeDtypeStruct((batch * seq, DIL_HEADS * hd), BF16),
        compiler_params=_params("parallel", "parallel", "arbitrary"),
        name="dil_attention",
    )(*([qkv] * (3 * n_groups)))


def _pad_last(x, width):
    return jnp.pad(x, [(0, 0)] * (x.ndim - 1) + [(0, width - x.shape[-1])])


def kernel(x, c, positions, w_cond, b_cond, w_mod, b_mod, g_mix_norm, g_ffn_norm, mla_w_in, mla_g_q_a, mla_g_kv_a, mla_w_q_b, mla_w_kv_b, mla_g_q_nope, mla_g_q_pe, mla_g_k_nope, mla_g_k_pe, mla_w_o, dil_w_qkv, dil_g_q, dil_g_k, dil_w_o, ffn_w_gate, ffn_w_up, ffn_w_down):
    batch, seq, d = x.shape
    depth = w_mod.shape[0]
    n_a = mla_w_in.shape[0]
    n_b = dil_w_qkv.shape[0]
    m = batch * seq
    hidden = ffn_w_gate.shape[-1]
    hidden_pad = -(-hidden // FFN_PAD_MULTIPLE) * FFN_PAD_MULTIPLE

    w_in = _pad_last(mla_w_in.astype(BF16), MLA_IN_PAD)
    qk_dim = MLA_NOPE_DIM + MLA_ROPE_DIM
    w_q_b = _pad_last(mla_w_q_b.astype(BF16).reshape(n_a, MLA_Q_RANK, MLA_HEADS, qk_dim), MLA_Q_HEAD_PAD)
    w_q_b = w_q_b.reshape(n_a, MLA_Q_RANK, MLA_HEADS * MLA_Q_HEAD_PAD)
    w_kv_b = mla_w_kv_b.astype(BF16)
    w_o_a = mla_w_o.astype(BF16)
    w_qkv = dil_w_qkv.astype(BF16)
    w_o_b = dil_w_o.astype(BF16)
    w_gate = _pad_last(ffn_w_gate.astype(BF16), hidden_pad)
    w_up = _pad_last(ffn_w_up.astype(BF16), hidden_pad)
    w_down = jnp.pad(ffn_w_down.astype(BF16), ((0, 0), (0, hidden_pad - hidden), (0, 0)))

    g_mix = g_mix_norm[:, None, :]
    g_ffn = g_ffn_norm[:, None, :]
    g_q_a = mla_g_q_a[:, None, :]
    g_kv_a = mla_g_kv_a[:, None, :]
    g_q_nope = mla_g_q_nope[:, None, :]
    g_k_nope = mla_g_k_nope[:, None, :]
    g_q_pe = _pad_last(mla_g_q_pe, LANES)[:, None, :]
    g_k_pe = _pad_last(mla_g_k_pe, LANES)[:, None, :]
    n_groups = len(DIL_GROUPS)
    dil_gain = jnp.stack([dil_g_q, dil_g_k, jnp.ones_like(dil_g_q)], axis=1)
    dil_gain = jnp.broadcast_to(dil_gain[:, :, :, None, :], (n_b, 3, n_groups, DIL_HEADS, DIL_HEAD_DIM))
    dil_gain = dil_gain.reshape(n_b, 1, 3 * n_groups * DIL_HEADS * DIL_HEAD_DIM)

    mod = _cond_mod(c, w_cond, b_cond, w_mod, b_mod)
    cos_m, s1_m, s2_m, cos_d, sin_d = _rope_tables(positions)

    xr = x.reshape(m, d)
    for i in range(depth):
        j = i // 2
        h = _norm_mod(xr, g_mix, mod, i, 0, 1, batch, seq)
        if i % 2 == 0:
            a = _matmul(h, w_in, j, tm=1024, tn=MLA_KV_BLOCK, out_dtype=F32)
            q = _mla_q_proj(a, g_q_a, w_q_b, j, g_q_nope, g_q_pe, (cos_m, s1_m, s2_m),
                            tm=1024, tn=1024)
            kv, kpe = _mla_kv_proj(a, g_kv_a, w_kv_b, j, g_k_nope, g_k_pe, (cos_m, s1_m, s2_m),
                                   tm=1024, tn=1024)
            o = _mla_attention(q, kv, kpe, batch, seq)
            xr = _matmul_resid(o, w_o_a, j, xr, mod, i, 2, batch, seq, tm=1024, tn=512)
        else:
            qkv = _dil_qkv_proj(h, w_qkv, j, dil_gain, (cos_d, sin_d), tm=1024, tn=1024)
            o = _dil_attention(qkv, batch, seq)
            xr = _matmul_resid(o, w_o_b, j, xr, mod, i, 2, batch, seq, tm=1024, tn=512)
        h = _norm_mod(xr, g_ffn, mod, i, 3, 4, batch, seq)
        act = _swiglu_up(h, w_gate, w_up, i, tm=1024, tn=512)
        xr = _matmul_resid(act, w_down, i, xr, mod, i, 5, batch, seq, tm=1024, tn=1024,
                           tk=hidden_pad // 4)
    return xr.reshape(batch, seq, d)
```

```python
import functools
import math

import jax
import jax.numpy as jnp
from jax import lax
from jax.experimental import pallas as pl
from jax.experimental.pallas import tpu as pltpu

F32 = jnp.float32
BF16 = jnp.bfloat16

ROPE_THETA = 10000.0
NORM_EPS = 1e-6
N_MOD = 6

MLA_HEADS = 32
MLA_Q_RANK = 1536
MLA_KV_RANK = 512
MLA_NOPE_DIM = 128
MLA_ROPE_DIM = 64
MLA_V_DIM = 128

DIL_GROUPS = ((128, 1), (512, 4), (2048, 16))
DIL_HEADS = 16
DIL_HEAD_DIM = 128

LANES = 128
V7X_VMEM_LIMIT_BYTES = 56 * 2**20

FFN_PAD_MULTIPLE = 1024
CAST_BLOCK = 256
MLA_IN_PAD = 2304
LOG2_E = math.log2(math.e)


def _params(*semantics):
    return pltpu.CompilerParams(dimension_semantics=semantics,
                                vmem_limit_bytes=V7X_VMEM_LIMIT_BYTES)


def _dot(a, b):
    return jnp.dot(a, b, preferred_element_type=F32)


def _dot_nt(a, b):
    return lax.dot_general(a, b, (((1,), (1,)), ((), ())), preferred_element_type=F32)


def _rms(x, gain, inv_n=None):
    if inv_n is None:
        ms = jnp.mean(x * x, axis=-1, keepdims=True)
    else:
        ms = jnp.sum(x * x, axis=-1, keepdims=True) * inv_n
    return x * lax.rsqrt(ms + NORM_EPS) * gain


def _cast_pad_kernel(axis, n_src_blocks, x_ref, o_ref):
    @pl.when(pl.program_id(axis) < n_src_blocks)
    def _():
        o_ref[...] = x_ref[...].astype(o_ref.dtype)

    @pl.when(pl.program_id(axis) >= n_src_blocks)
    def _():
        o_ref[...] = jnp.zeros_like(o_ref)


def _cast_pad(w, axis, padded):
    layers, rows, cols = w.shape
    n_src = w.shape[axis] // CAST_BLOCK
    n_dst = padded // CAST_BLOCK
    if axis == 2:
        block = (None, rows, CAST_BLOCK)
        in_map = lambda l, j: (l, 0, jnp.minimum(j, n_src - 1))
        out_map = lambda l, j: (l, 0, j)
        out_shape = (layers, rows, padded)
    else:
        block = (None, CAST_BLOCK, cols)
        in_map = lambda l, j: (l, jnp.minimum(j, n_src - 1), 0)
        out_map = lambda l, j: (l, j, 0)
        out_shape = (layers, padded, cols)
    return pl.pallas_call(
        functools.partial(_cast_pad_kernel, 1, n_src),
        grid=(layers, n_dst),
        in_specs=[pl.BlockSpec(block, in_map)],
        out_specs=pl.BlockSpec(block, out_map),
        out_shape=jax.ShapeDtypeStruct(out_shape, BF16),
        compiler_params=_params("parallel", "arbitrary"),
        name="cast_pad",
    )(w)


def _cond_mod_kernel(c_ref, wc_ref, bc_ref, wm_ref, bm_ref, o_ref, e_ref):
    @pl.when((pl.program_id(0) == 0) & (pl.program_id(1) == 0))
    def _():
        z = _dot(c_ref[...].astype(BF16), wc_ref[...].astype(BF16)) + bc_ref[...]
        e_ref[...] = z * jax.nn.sigmoid(z)

    o_ref[...] = _dot(e_ref[...].astype(BF16), wm_ref[...].astype(BF16)) + bm_ref[...]


def _cond_mod(c, w_cond, b_cond, w_mod, b_mod):
    batch, d = c.shape
    depth, rank, n = w_mod.shape
    rows = 8
    tn = 2048
    c8 = jnp.pad(c, ((0, rows - batch), (0, 0)))
    out = pl.pallas_call(
        _cond_mod_kernel,
        grid=(depth, n // tn),
        in_specs=[
            pl.BlockSpec((rows, d), lambda l, j: (0, 0)),
            pl.BlockSpec((d, rank), lambda l, j: (0, 0)),
            pl.BlockSpec((1, rank), lambda l, j: (0, 0)),
            pl.BlockSpec((None, rank, tn), lambda l, j: (l, 0, j)),
            pl.BlockSpec((None, 1, tn), lambda l, j: (l, 0, j)),
        ],
        out_specs=pl.BlockSpec((None, rows, tn), lambda l, j: (l, 0, j)),
        out_shape=jax.ShapeDtypeStruct((depth, rows, n), F32),
        scratch_shapes=[pltpu.VMEM((rows, rank), F32)],
        compiler_params=_params("arbitrary", "arbitrary"),
        name="cond_mod",
    )(c8, w_cond, b_cond.reshape(1, rank), w_mod, b_mod.reshape(depth, 1, n))
    return out[:, :batch].reshape(depth * batch, 1, n)


def _rope_tables_kernel(pos_ref, cm_ref, s1_ref, s2_ref, cd_ref, sd_ref):
    pos = pos_ref[...].astype(F32)
    lane = lax.broadcasted_iota(jnp.int32, (1, LANES), 1)

    def angles(dim):
        half = dim // 2
        idx = (lane & (half - 1)).astype(F32)
        inv_freq = jnp.exp(idx * (-2.0 * math.log(ROPE_THETA) / dim))
        return pos * inv_freq

    ang = angles(MLA_ROPE_DIM)
    cos, sin = jnp.cos(ang), jnp.sin(ang)
    first_half = (lane & (MLA_ROPE_DIM - 1)) < MLA_ROPE_DIM // 2
    cm_ref[...] = cos
    s1_ref[...] = jnp.where(first_half, -sin, 0.0)
    s2_ref[...] = jnp.where(first_half, 0.0, sin)
    ang = angles(DIL_HEAD_DIM)
    cos, sin = jnp.cos(ang), jnp.sin(ang)
    cd_ref[...] = cos
    sd_ref[...] = jnp.where(lane < DIL_HEAD_DIM // 2, -sin, sin)


def _rope_tables(positions):
    batch, seq = positions.shape
    ts = 512
    m = batch * seq
    table = jax.ShapeDtypeStruct((m, LANES), F32)
    spec = pl.BlockSpec((ts, LANES), lambda i: (i, 0))
    return pl.pallas_call(
        _rope_tables_kernel,
        grid=(m // ts,),
        in_specs=[pl.BlockSpec((ts, 1), lambda i: (i, 0))],
        out_specs=[spec] * 5,
        out_shape=[table] * 5,
        compiler_params=_params("parallel"),
        name="rope_tables",
    )(positions.reshape(m, 1))


def _norm_mod_kernel(x_ref, g_ref, sc_ref, sh_ref, o_ref):
    y = _rms(x_ref[...], g_ref[...])
    o_ref[...] = (y * (1.0 + sc_ref[...]) + sh_ref[...]).astype(o_ref.dtype)


def _norm_mod(x, gain, mod, layer, shift_idx, scale_idx, batch, seq):
    m, d = x.shape
    tm = 512
    per_batch = seq // tm

    def mod_spec(which):
        return pl.BlockSpec((None, 1, d), lambda i: (layer * batch + i // per_batch, 0, which))

    return pl.pallas_call(
        _norm_mod_kernel,
        grid=(m // tm,),
        in_specs=[
            pl.BlockSpec((tm, d), lambda i: (i, 0)),
            pl.BlockSpec((None, 1, d), lambda i: (layer, 0, 0)),
            mod_spec(scale_idx),
            mod_spec(shift_idx),
        ],
        out_specs=pl.BlockSpec((tm, d), lambda i: (i, 0)),
        out_shape=jax.ShapeDtypeStruct((m, d), BF16),
        compiler_params=_params("parallel"),
        name="norm_mod",
    )(x, gain, mod, mod)


def _mm_kernel(x_ref, w_ref, o_ref):
    o_ref[...] = _dot(x_ref[...], w_ref[...]).astype(o_ref.dtype)


def _matmul(x, w, layer, *, tm, tn, out_dtype):
    m, k = x.shape
    n = w.shape[-1]
    return pl.pallas_call(
        _mm_kernel,
        grid=(m // tm, n // tn),
        in_specs=[
            pl.BlockSpec((tm, k), lambda i, j: (i, 0)),
            pl.BlockSpec((None, k, tn), lambda i, j: (layer, 0, j)),
        ],
        out_specs=pl.BlockSpec((tm, tn), lambda i, j: (i, j)),
        out_shape=jax.ShapeDtypeStruct((m, n), out_dtype),
        compiler_params=_params("parallel", "arbitrary"),
        name="matmul",
    )(x, w)


def _mm_resid_kernel(x_ref, w_ref, r_ref, gt_ref, o_ref):
    o_ref[...] = r_ref[...] + gt_ref[...] * _dot(x_ref[...], w_ref[...])


def _mm_resid_ksplit_kernel(x_ref, w_ref, r_ref, gt_ref, o_ref, acc_ref):
    kk = pl.program_id(2)

    @pl.when(kk == 0)
    def _():
        acc_ref[...] = jnp.zeros_like(acc_ref)

    acc_ref[...] += _dot(x_ref[...], w_ref[...])

    @pl.when(kk == pl.num_programs(2) - 1)
    def _():
        o_ref[...] = r_ref[...] + gt_ref[...] * acc_ref[...]


def _matmul_resid(x, w, layer, resid, mod, mod_layer, gate_idx, batch, seq, *, tm, tn, tk=None):
    m, k = x.shape
    n = w.shape[-1]
    per_batch = seq // tm
    gate_blocks = n // tn
    if tk is None:
        return pl.pallas_call(
            _mm_resid_kernel,
            grid=(m // tm, n // tn),
            in_specs=[
                pl.BlockSpec((tm, k), lambda i, j: (i, 0)),
                pl.BlockSpec((None, k, tn), lambda i, j: (layer, 0, j)),
                pl.BlockSpec((tm, tn), lambda i, j: (i, j)),
                pl.BlockSpec((None, 1, tn), lambda i, j: (mod_layer * batch + i // per_batch, 0,
                                                          gate_idx * gate_blocks + j)),
            ],
            out_specs=pl.BlockSpec((tm, tn), lambda i, j: (i, j)),
            out_shape=jax.ShapeDtypeStruct((m, n), F32),
            compiler_params=_params("parallel", "arbitrary"),
            name="matmul_resid",
        )(x, w, resid, mod)
    return pl.pallas_call(
        _mm_resid_ksplit_kernel,
        grid=(m // tm, n // tn, k // tk),
        in_specs=[
            pl.BlockSpec((tm, tk), lambda i, j, kk: (i, kk)),
            pl.BlockSpec((None, tk, tn), lambda i, j, kk: (layer, kk, j)),
            pl.BlockSpec((tm, tn), lambda i, j, kk: (i, j)),
            pl.BlockSpec((None, 1, tn), lambda i, j, kk: (mod_layer * batch + i // per_batch, 0,
                                                          gate_idx * gate_blocks + j)),
        ],
        out_specs=pl.BlockSpec((tm, tn), lambda i, j, kk: (i, j)),
        out_shape=jax.ShapeDtypeStruct((m, n), F32),
        scratch_shapes=[pltpu.VMEM((tm, tn), F32)],
        compiler_params=_params("parallel", "parallel", "arbitrary"),
        name="matmul_resid_ksplit",
    )(x, w, resid, mod)


def _swiglu_kernel(x_ref, wg_ref, wu_ref, o_ref):
    x = x_ref[...]
    g = _dot(x, wg_ref[...])
    u = _dot(x, wu_ref[...])
    o_ref[...] = (g * jax.nn.sigmoid(g) * u).astype(o_ref.dtype)


def _swiglu_up(x, wg, wu, layer, *, tm, tn):
    m, k = x.shape
    n = wg.shape[-1]
    w_spec = pl.BlockSpec((None, k, tn), lambda i, j: (layer, 0, j))
    return pl.pallas_call(
        _swiglu_kernel,
        grid=(m // tm, n // tn),
        in_specs=[pl.BlockSpec((tm, k), lambda i, j: (i, 0)), w_spec, w_spec],
        out_specs=pl.BlockSpec((tm, tn), lambda i, j: (i, j)),
        out_shape=jax.ShapeDtypeStruct((m, n), BF16),
        compiler_params=_params("parallel", "arbitrary"),
        name="swiglu_up",
    )(x, wg, wu)


def _rope_mla(x, cos, s1, s2):
    half = MLA_ROPE_DIM // 2
    return (x * cos + pltpu.roll(x, LANES - half, 1) * s1 + pltpu.roll(x, half, 1) * s2)


def _rope_dil(x, cos, sin_signed):
    return x * cos + pltpu.roll(x, DIL_HEAD_DIM // 2, 1) * sin_signed


def _mla_q_kernel(n_nope_blocks, cq_ref, gqa_ref, w_ref, gn_ref, gp_ref, cos_ref, s1_ref, s2_ref,
                  o_ref, hs_ref):
    j = pl.program_id(1)

    @pl.when(j == 0)
    def _():
        hs_ref[...] = _rms(cq_ref[...], gqa_ref[...]).astype(hs_ref.dtype)

    acc = _dot(hs_ref[...], w_ref[...])
    n_tiles = acc.shape[1] // LANES

    @pl.when(j < n_nope_blocks)
    def _():
        for t in range(n_tiles):
            c0 = t * LANES
            o_ref[:, c0:c0 + LANES] = _rms(acc[:, c0:c0 + LANES], gn_ref[...]).astype(o_ref.dtype)

    @pl.when(j >= n_nope_blocks)
    def _():
        cos, s1, s2 = cos_ref[...], s1_ref[...], s2_ref[...]
        low = lax.broadcasted_iota(jnp.int32, (1, LANES), 1) < MLA_ROPE_DIM
        for t in range(n_tiles):
            c0 = t * LANES
            x = acc[:, c0:c0 + LANES]
            sq = x * x
            ss_low = jnp.sum(jnp.where(low, sq, 0.0), axis=-1, keepdims=True)
            ss_high = jnp.sum(jnp.where(low, 0.0, sq), axis=-1, keepdims=True)
            ms = jnp.where(low, ss_low, ss_high) * (1.0 / MLA_ROPE_DIM)
            y = x * lax.rsqrt(ms + NORM_EPS) * gp_ref[...]
            o_ref[:, c0:c0 + LANES] = _rope_mla(y, cos, s1, s2).astype(o_ref.dtype)


def _mla_q_proj(a, g_q_a, w, layer, g_nope, g_pe_pair, tables, *, tm, tn):
    m = a.shape[0]
    k, n = w.shape[1:]
    cos, s1, s2 = tables
    tab = pl.BlockSpec((tm, LANES), lambda i, j: (i, 0))
    n_nope_blocks = (MLA_HEADS * MLA_NOPE_DIM) // tn
    return pl.pallas_call(
        functools.partial(_mla_q_kernel, n_nope_blocks),
        grid=(m // tm, n // tn),
        in_specs=[
            pl.BlockSpec((tm, k), lambda i, j: (i, 0)),
            pl.BlockSpec((None, 1, k), lambda i, j: (layer, 0, 0)),
            pl.BlockSpec((None, k, tn), lambda i, j: (layer, 0, j)),
            pl.BlockSpec((None, 1, LANES), lambda i, j: (layer, 0, 0)),
            pl.BlockSpec((None, 1, LANES), lambda i, j: (layer, 0, 0)),
            tab, tab, tab,
        ],
        out_specs=pl.BlockSpec((tm, tn), lambda i, j: (i, j)),
        out_shape=jax.ShapeDtypeStruct((m, n), BF16),
        scratch_shapes=[pltpu.VMEM((tm, k), BF16)],
        compiler_params=_params("parallel", "arbitrary"),
        name="mla_q_proj",
    )(a, g_q_a, w, g_nope, g_pe_pair, cos, s1, s2)


MLA_KV_BLOCK = 768


def _mla_kv_kernel(ckv_ref, gkva_ref, w_ref, gk_ref, gp_ref, cos_ref, s1_ref, s2_ref,
                   kv_ref, kpe_ref, hs_ref):
    @pl.when(pl.program_id(1) == 0)
    def _():
        blk = ckv_ref[...]
        hs_ref[...] = _rms(blk[:, :MLA_KV_RANK], gkva_ref[...]).astype(hs_ref.dtype)
        pe = _rms(blk[:, MLA_KV_RANK:MLA_KV_RANK + LANES], gp_ref[...], inv_n=1.0 / MLA_ROPE_DIM)
        pe = _rope_mla(pe, cos_ref[...], s1_ref[...], s2_ref[...])
        kpe_ref[:, :LANES] = pe.astype(kpe_ref.dtype)
        kpe_ref[:, LANES:] = pltpu.roll(pe, MLA_ROPE_DIM, 1).astype(kpe_ref.dtype)

    acc = _dot(hs_ref[...], w_ref[...])
    head_w = MLA_NOPE_DIM + MLA_V_DIM
    for hh in range(acc.shape[1] // head_w):
        c0 = hh * head_w
        kv_ref[:, c0:c0 + MLA_NOPE_DIM] = _rms(acc[:, c0:c0 + MLA_NOPE_DIM], gk_ref[...]).astype(kv_ref.dtype)
        kv_ref[:, c0 + MLA_NOPE_DIM:c0 + head_w] = acc[:, c0 + MLA_NOPE_DIM:c0 + head_w].astype(kv_ref.dtype)


def _mla_kv_proj(a, g_kv_a, w, layer, g_k_nope, g_pe_pad, tables, *, tm, tn):
    m = a.shape[0]
    k, n = w.shape[1:]
    cos, s1, s2 = tables
    tab = pl.BlockSpec((tm, LANES), lambda i, j: (i, 0))
    kv_col_block = MLA_Q_RANK // MLA_KV_BLOCK
    return pl.pallas_call(
        _mla_kv_kernel,
        grid=(m // tm, n // tn),
        in_specs=[
            pl.BlockSpec((tm, MLA_KV_BLOCK), lambda i, j: (i, kv_col_block)),
            pl.BlockSpec((None, 1, k), lambda i, j: (layer, 0, 0)),
            pl.BlockSpec((None, k, tn), lambda i, j: (layer, 0, j)),
            pl.BlockSpec((None, 1, LANES), lambda i, j: (layer, 0, 0)),
            pl.BlockSpec((None, 1, LANES), lambda i, j: (layer, 0, 0)),
            tab, tab, tab,
        ],
        out_specs=[
            pl.BlockSpec((tm, tn), lambda i, j: (i, j)),
            pl.BlockSpec((tm, 2 * LANES), lambda i, j: (i, 0)),
        ],
        out_shape=[
            jax.ShapeDtypeStruct((m, n), BF16),
            jax.ShapeDtypeStruct((m, 2 * LANES), BF16),
        ],
        scratch_shapes=[pltpu.VMEM((tm, k), BF16)],
        compiler_params=_params("parallel", "arbitrary"),
        name="mla_kv_proj",
    )(a, g_kv_a, w, g_k_nope, g_pe_pad, cos, s1, s2)


def _softmax_pv(scores, values, scale):
    c = scale * LOG2_E
    m = None
    for s in scores:
        mi = jnp.max(s, axis=-1, keepdims=True)
        m = mi if m is None else jnp.maximum(m, mi)
    mc = m * c
    den = None
    acc = None
    for s, v in zip(scores, values):
        p = jnp.exp2(s * c - mc)
        li = jnp.sum(p, axis=-1, keepdims=True)
        ai = _dot(p.astype(v.dtype), v)
        den = li if den is None else den + li
        acc = ai if acc is None else acc + ai
    return acc / den


MLA_Q_BLOCK = 512


def _mla_attn_kernel(qn_ref, qpe_ref, kv_ref, kpe_ref, o_ref, kcat_ref):
    seq = qn_ref.shape[0]
    tq = MLA_Q_BLOCK
    scale = (MLA_NOPE_DIM + MLA_ROPE_DIM) ** -0.5
    head_w = MLA_NOPE_DIM + MLA_V_DIM
    for hh in range(2):
        kcat_ref[hh, :, :MLA_NOPE_DIM] = kv_ref[:, hh * head_w:hh * head_w + MLA_NOPE_DIM]
        kcat_ref[hh, :, MLA_NOPE_DIM:] = kpe_ref[:, hh * LANES:(hh + 1) * LANES]

    row = lax.broadcasted_iota(jnp.int32, (tq, tq), 0)
    col = lax.broadcasted_iota(jnp.int32, (tq, tq), 1)
    causal = col <= row
    for qb in range(seq // tq):
        lo = qb * tq
        qpe = qpe_ref[lo:lo + tq, :]
        for hh in range(2):
            q = jnp.concatenate([qn_ref[lo:lo + tq, hh * LANES:(hh + 1) * LANES], qpe], axis=1)
            v0 = hh * head_w + MLA_NOPE_DIM
            s_diag = jnp.where(causal, _dot_nt(q, kcat_ref[hh, lo:lo + tq, :]), -jnp.inf)
            scores = [s_diag]
            values = [kv_ref[lo:lo + tq, v0:v0 + MLA_V_DIM]]
            if qb > 0:
                scores.append(_dot_nt(q, kcat_ref[hh, 0:lo, :]))
                values.append(kv_ref[0:lo, v0:v0 + MLA_V_DIM])
            o = _softmax_pv(scores, values, scale)
            o_ref[lo:lo + tq, hh * MLA_V_DIM:(hh + 1) * MLA_V_DIM] = o.astype(o_ref.dtype)


def _mla_attention(q, kv, kpe, batch, seq):
    pairs = MLA_HEADS // 2
    nope_tiles = MLA_HEADS * MLA_NOPE_DIM // LANES
    return pl.pallas_call(
        _mla_attn_kernel,
        grid=(batch, pairs),
        in_specs=[
            pl.BlockSpec((seq, 2 * MLA_NOPE_DIM), lambda b, p: (b, p)),
            pl.BlockSpec((seq, LANES), lambda b, p: (b, nope_tiles + p)),
            pl.BlockSpec((seq, 2 * (MLA_NOPE_DIM + MLA_V_DIM)), lambda b, p: (b, p)),
            pl.BlockSpec((seq, 2 * LANES), lambda b, p: (b, 0)),
        ],
        out_specs=pl.BlockSpec((seq, 2 * MLA_V_DIM), lambda b, p: (b, p)),
        out_shape=jax.ShapeDtypeStruct((batch * seq, MLA_HEADS * MLA_V_DIM), BF16),
        scratch_shapes=[pltpu.VMEM((2, seq, MLA_NOPE_DIM + LANES), BF16)],
        compiler_params=_params("parallel", "parallel"),
        name="mla_attention",
    )(q, q, kv, kpe)


def _dil_qkv_kernel(n_qk_blocks, x_ref, w_ref, gain_ref, cos_ref, sin_ref, o_ref):
    acc = _dot(x_ref[...], w_ref[...])
    j = pl.program_id(1)

    @pl.when(j < n_qk_blocks)
    def _():
        cos, sin = cos_ref[...], sin_ref[...]
        for hh in range(acc.shape[1] // DIL_HEAD_DIM):
            c0 = hh * DIL_HEAD_DIM
            y = _rms(acc[:, c0:c0 + DIL_HEAD_DIM], gain_ref[:, c0:c0 + DIL_HEAD_DIM])
            o_ref[:, c0:c0 + DIL_HEAD_DIM] = _rope_dil(y, cos, sin).astype(o_ref.dtype)

    @pl.when(j >= n_qk_blocks)
    def _():
        o_ref[...] = acc.astype(o_ref.dtype)


def _dil_qkv_proj(x, w, layer, gain_cols, tables, *, tm, tn):
    m, k = x.shape
    n = w.shape[-1]
    cos, sin = tables
    n_qk_blocks = (2 * n // 3) // tn
    tab = pl.BlockSpec((tm, LANES), lambda i, j: (i, 0))
    return pl.pallas_call(
        functools.partial(_dil_qkv_kernel, n_qk_blocks),
        grid=(m // tm, n // tn),
        in_specs=[
            pl.BlockSpec((tm, k), lambda i, j: (i, 0)),
            pl.BlockSpec((None, k, tn), lambda i, j: (layer, 0, j)),
            pl.BlockSpec((None, 1, tn), lambda i, j: (layer, 0, j)),
            tab, tab,
        ],
        out_specs=pl.BlockSpec((tm, tn), lambda i, j: (i, j)),
        out_shape=jax.ShapeDtypeStruct((m, n), BF16),
        compiler_params=_params("parallel", "arbitrary"),
        name="dil_qkv_proj",
    )(x, w, gain_cols, cos, sin)


DIL_Q_BLOCK = 128


def _dil_bias(rows, cols, offset, window, dilation):
    dist = (offset + lax.broadcasted_iota(jnp.int32, (rows, cols), 0)
            - lax.broadcasted_iota(jnp.int32, (rows, cols), 1))
    valid = (dist >= 0) & (dist <= window) & ((dist & (dilation - 1)) == 0)
    return jnp.where(valid, 0.0, -jnp.inf).astype(F32)


def _dil_attn_kernel(*refs):
    n_groups = len(DIL_GROUPS)
    q_refs = refs[:n_groups]
    k_refs = refs[n_groups:2 * n_groups]
    v_refs = refs[2 * n_groups:3 * n_groups]
    o_ref = refs[3 * n_groups]
    seq = o_ref.shape[0]
    tq = DIL_Q_BLOCK
    scale = DIL_HEAD_DIM ** -0.5
    diag, full, cut = [], [], []
    for window, dilation in DIL_GROUPS:
        assert tq % dilation == 0 and window % tq == 0
        diag.append(_dil_bias(tq, tq, 0, window, dilation))
        full.append(_dil_bias(tq, window, window, window, dilation) if window < seq else None)
        n_cut = min(window, seq) - tq
        cut.append(_dil_bias(tq, n_cut, n_cut, seq, dilation) if n_cut > 0 and dilation > 1 else None)

    for sb in range(seq // tq):
        t0 = sb * tq
        scores, values = [], []
        for g, (window, dilation) in enumerate(DIL_GROUPS):
            q = q_refs[g][t0:t0 + tq, :]
            scores.append(_dot_nt(q, k_refs[g][t0:t0 + tq, :]) + diag[g])
            values.append(v_refs[g][t0:t0 + tq, :])
            n_prev = min(window, t0)
            if n_prev == 0:
                continue
            s_prev = _dot_nt(q, k_refs[g][t0 - n_prev:t0, :])
            if n_prev == window:
                s_prev = s_prev + full[g]
            elif cut[g] is not None:
                s_prev = s_prev + cut[g][:, cut[g].shape[1] - n_prev:]
            scores.append(s_prev)
            values.append(v_refs[g][t0 - n_prev:t0, :])
        o = _softmax_pv(scores, values, scale)
        o_ref[t0:t0 + tq, :] = o.astype(o_ref.dtype)


def _dil_attention(qkv, batch, seq):
    n_groups = len(DIL_GROUPS)
    hd = DIL_HEAD_DIM

    def spec(part, g):
        c = (part * n_groups + g) * DIL_HEADS
        return pl.BlockSpec((seq, hd), lambda b, h: (b, c + h))

    return pl.pallas_call(
        _dil_attn_kernel,
        grid=(batch, DIL_HEADS),
        in_specs=[spec(part, g) for part in range(3) for g in range(n_groups)],
        out_specs=pl.BlockSpec((seq, hd), lambda b, h: (b, h)),
        out_shape=jax.ShapeDtypeStruct((batch * seq, DIL_HEADS * hd), BF16),
        compiler_params=_params("parallel", "parallel"),
        name="dil_attention",
    )(*([qkv] * (3 * n_groups)))


def _pad_last(x, width):
    return jnp.pad(x, [(0, 0)] * (x.ndim - 1) + [(0, width - x.shape[-1])])


def kernel(x, c, positions, w_cond, b_cond, w_mod, b_mod, g_mix_norm, g_ffn_norm, mla_w_in, mla_g_q_a, mla_g_kv_a, mla_w_q_b, mla_w_kv_b, mla_g_q_nope, mla_g_q_pe, mla_g_k_nope, mla_g_k_pe, mla_w_o, dil_w_qkv, dil_g_q, dil_g_k, dil_w_o, ffn_w_gate, ffn_w_up, ffn_w_down):
    batch, seq, d = x.shape
    depth = w_mod.shape[0]
    n_a = mla_w_in.shape[0]
    n_b = dil_w_qkv.shape[0]
    m = batch * seq
    hidden = ffn_w_gate.shape[-1]
    hidden_pad = -(-hidden // FFN_PAD_MULTIPLE) * FFN_PAD_MULTIPLE

    w_in = _pad_last(mla_w_in.astype(BF16), MLA_IN_PAD)
    qk_dim = MLA_NOPE_DIM + MLA_ROPE_DIM
    w_q_b = mla_w_q_b.astype(BF16).reshape(n_a, MLA_Q_RANK, MLA_HEADS, qk_dim)
    w_q_b = jnp.concatenate(
        [w_q_b[..., :MLA_NOPE_DIM].reshape(n_a, MLA_Q_RANK, MLA_HEADS * MLA_NOPE_DIM),
         w_q_b[..., MLA_NOPE_DIM:].reshape(n_a, MLA_Q_RANK, MLA_HEADS * MLA_ROPE_DIM)], axis=-1)
    w_kv_b = mla_w_kv_b.astype(BF16)
    w_o_a = mla_w_o.astype(BF16)
    w_qkv = dil_w_qkv.astype(BF16)
    w_o_b = dil_w_o.astype(BF16)
    w_gate = _cast_pad(ffn_w_gate, 2, hidden_pad)
    w_up = _cast_pad(ffn_w_up, 2, hidden_pad)
    w_down = _cast_pad(ffn_w_down, 1, hidden_pad)

    g_mix = g_mix_norm[:, None, :]
    g_ffn = g_ffn_norm[:, None, :]
    g_q_a = mla_g_q_a[:, None, :]
    g_kv_a = mla_g_kv_a[:, None, :]
    g_q_nope = mla_g_q_nope[:, None, :]
    g_k_nope = mla_g_k_nope[:, None, :]
    g_q_pe = jnp.concatenate([mla_g_q_pe, mla_g_q_pe], axis=-1)[:, None, :]
    g_k_pe = _pad_last(mla_g_k_pe, LANES)[:, None, :]
    n_groups = len(DIL_GROUPS)
    dil_gain = jnp.stack([dil_g_q, dil_g_k, jnp.ones_like(dil_g_q)], axis=1)
    dil_gain = jnp.broadcast_to(dil_gain[:, :, :, None, :], (n_b, 3, n_groups, DIL_HEADS, DIL_HEAD_DIM))
    dil_gain = dil_gain.reshape(n_b, 1, 3 * n_groups * DIL_HEADS * DIL_HEAD_DIM)

    mod = _cond_mod(c, w_cond, b_cond, w_mod, b_mod)
    cos_m, s1_m, s2_m, cos_d, sin_d = _rope_tables(positions)

    xr = x.reshape(m, d)
    for i in range(depth):
        j = i // 2
        h = _norm_mod(xr, g_mix, mod, i, 0, 1, batch, seq)
        if i % 2 == 0:
            a = _matmul(h, w_in, j, tm=1024, tn=MLA_KV_BLOCK, out_dtype=F32)
            q = _mla_q_proj(a, g_q_a, w_q_b, j, g_q_nope, g_q_pe, (cos_m, s1_m, s2_m),
                            tm=1024, tn=1024)
            kv, kpe = _mla_kv_proj(a, g_kv_a, w_kv_b, j, g_k_nope, g_k_pe, (cos_m, s1_m, s2_m),
                                   tm=1024, tn=1024)
            o = _mla_attention(q, kv, kpe, batch, seq)
            xr = _matmul_resid(o, w_o_a, j, xr, mod, i, 2, batch, seq, tm=1024, tn=512)
        else:
            qkv = _dil_qkv_proj(h, w_qkv, j, dil_gain, (cos_d, sin_d), tm=1024, tn=1024)
            o = _dil_attention(qkv, batch, seq)
            xr = _matmul_resid(o, w_o_b, j, xr, mod, i, 2, batch, seq, tm=1024, tn=512)
        h = _norm_mod(xr, g_ffn, mod, i, 3, 4, batch, seq)
        act = _swiglu_up(h, w_gate, w_up, i, tm=1024, tn=512)
        xr = _matmul_resid(act, w_down, i, xr, mod, i, 5, batch, seq, tm=1024, tn=1024,
                           tk=hidden_pad // 4)
    return xr.reshape(batch, seq, d)
```

```python
import functools
import math

import jax
import jax.numpy as jnp
from jax import lax
from jax.experimental import pallas as pl
from jax.experimental.pallas import tpu as pltpu

F32 = jnp.float32
BF16 = jnp.bfloat16

ROPE_THETA = 10000.0
NORM_EPS = 1e-6
N_MOD = 6

MLA_HEADS = 32
MLA_Q_RANK = 1536
MLA_KV_RANK = 512
MLA_NOPE_DIM = 128
MLA_ROPE_DIM = 64
MLA_V_DIM = 128

DIL_GROUPS = ((128, 1), (512, 4), (2048, 16))
DIL_HEADS = 16
DIL_HEAD_DIM = 128

LANES = 128
V7X_VMEM_LIMIT_BYTES = 56 * 2**20

FFN_PAD_MULTIPLE = 1024
CAST_BLOCK = 256
MLA_IN_PAD = 2304
LOG2_E = math.log2(math.e)


def _params(*semantics):
    return pltpu.CompilerParams(dimension_semantics=semantics,
                                vmem_limit_bytes=V7X_VMEM_LIMIT_BYTES)


def _dot(a, b):
    return jnp.dot(a, b, preferred_element_type=F32)


def _dot_nt(a, b):
    return lax.dot_general(a, b, (((1,), (1,)), ((), ())), preferred_element_type=F32)


def _rms(x, gain, inv_n=None):
    if inv_n is None:
        ms = jnp.mean(x * x, axis=-1, keepdims=True)
    else:
        ms = jnp.sum(x * x, axis=-1, keepdims=True) * inv_n
    return x * lax.rsqrt(ms + NORM_EPS) * gain


def _cast_pad_kernel(axis, n_src_blocks, x_ref, o_ref):
    @pl.when(pl.program_id(axis) < n_src_blocks)
    def _():
        o_ref[...] = x_ref[...].astype(o_ref.dtype)

    @pl.when(pl.program_id(axis) >= n_src_blocks)
    def _():
        o_ref[...] = jnp.zeros_like(o_ref)


def _cast_pad(w, axis, padded):
    layers, rows, cols = w.shape
    n_src = w.shape[axis] // CAST_BLOCK
    n_dst = padded // CAST_BLOCK
    if axis == 2:
        block = (layers, rows, CAST_BLOCK)
        in_map = lambda j: (0, 0, jnp.minimum(j, n_src - 1))
        out_map = lambda j: (0, 0, j)
        out_shape = (layers, rows, padded)
    else:
        block = (layers, CAST_BLOCK, cols)
        in_map = lambda j: (0, jnp.minimum(j, n_src - 1), 0)
        out_map = lambda j: (0, j, 0)
        out_shape = (layers, padded, cols)
    return pl.pallas_call(
        functools.partial(_cast_pad_kernel, 0, n_src),
        grid=(n_dst,),
        in_specs=[pl.BlockSpec(block, in_map)],
        out_specs=pl.BlockSpec(block, out_map),
        out_shape=jax.ShapeDtypeStruct(out_shape, BF16),
        compiler_params=_params("arbitrary"),
        name="cast_pad",
    )(w)


def _cond_mod_kernel(c_ref, wc_ref, bc_ref, wm_ref, bm_ref, o_ref, e_ref):
    @pl.when((pl.program_id(0) == 0) & (pl.program_id(1) == 0))
    def _():
        z = _dot(c_ref[...].astype(BF16), wc_ref[...].astype(BF16)) + bc_ref[...]
        e_ref[...] = z * jax.nn.sigmoid(z)

    o_ref[...] = _dot(e_ref[...].astype(BF16), wm_ref[...].astype(BF16)) + bm_ref[...]


def _cond_mod(c, w_cond, b_cond, w_mod, b_mod):
    batch, d = c.shape
    depth, rank, n = w_mod.shape
    rows = 8
    tn = 2048
    c8 = jnp.pad(c, ((0, rows - batch), (0, 0)))
    out = pl.pallas_call(
        _cond_mod_kernel,
        grid=(depth, n // tn),
        in_specs=[
            pl.BlockSpec((rows, d), lambda l, j: (0, 0)),
            pl.BlockSpec((d, rank), lambda l, j: (0, 0)),
            pl.BlockSpec((1, rank), lambda l, j: (0, 0)),
            pl.BlockSpec((None, rank, tn), lambda l, j: (l, 0, j)),
            pl.BlockSpec((None, 1, tn), lambda l, j: (l, 0, j)),
        ],
        out_specs=pl.BlockSpec((None, rows, tn), lambda l, j: (l, 0, j)),
        out_shape=jax.ShapeDtypeStruct((depth, rows, n), F32),
        scratch_shapes=[pltpu.VMEM((rows, rank), F32)],
        compiler_params=_params("arbitrary", "arbitrary"),
        name="cond_mod",
    )(c8, w_cond, b_cond.reshape(1, rank), w_mod, b_mod.reshape(depth, 1, n))
    return out[:, :batch].reshape(depth * batch, 1, n)


def _rope_tables_kernel(pos_ref, cm_ref, s1_ref, s2_ref, cd_ref, sd_ref):
    pos = pos_ref[...].astype(F32)
    lane = lax.broadcasted_iota(jnp.int32, (1, LANES), 1)

    def angles(dim):
        half = dim // 2
        idx = (lane & (half - 1)).astype(F32)
        inv_freq = jnp.exp(idx * (-2.0 * math.log(ROPE_THETA) / dim))
        return pos * inv_freq

    ang = angles(MLA_ROPE_DIM)
    cos, sin = jnp.cos(ang), jnp.sin(ang)
    first_half = (lane & (MLA_ROPE_DIM - 1)) < MLA_ROPE_DIM // 2
    cm_ref[...] = cos
    s1_ref[...] = jnp.where(first_half, -sin, 0.0)
    s2_ref[...] = jnp.where(first_half, 0.0, sin)
    ang = angles(DIL_HEAD_DIM)
    cos, sin = jnp.cos(ang), jnp.sin(ang)
    cd_ref[...] = cos
    sd_ref[...] = jnp.where(lane < DIL_HEAD_DIM // 2, -sin, sin)


def _rope_tables(positions):
    batch, seq = positions.shape
    ts = 512
    m = batch * seq
    table = jax.ShapeDtypeStruct((m, LANES), F32)
    spec = pl.BlockSpec((ts, LANES), lambda i: (i, 0))
    return pl.pallas_call(
        _rope_tables_kernel,
        grid=(m // ts,),
        in_specs=[pl.BlockSpec((ts, 1), lambda i: (i, 0))],
        out_specs=[spec] * 5,
        out_shape=[table] * 5,
        compiler_params=_params("parallel"),
        name="rope_tables",
    )(positions.reshape(m, 1))


def _norm_mod_kernel(x_ref, g_ref, sc_ref, sh_ref, o_ref):
    y = _rms(x_ref[...], g_ref[...])
    o_ref[...] = (y * (1.0 + sc_ref[...]) + sh_ref[...]).astype(o_ref.dtype)


def _norm_mod(x, gain, mod, layer, shift_idx, scale_idx, batch, seq):
    m, d = x.shape
    tm = 512
    per_batch = seq // tm

    def mod_spec(which):
        return pl.BlockSpec((None, 1, d), lambda i: (layer * batch + i // per_batch, 0, which))

    return pl.pallas_call(
        _norm_mod_kernel,
        grid=(m // tm,),
        in_specs=[
            pl.BlockSpec((tm, d), lambda i: (i, 0)),
            pl.BlockSpec((None, 1, d), lambda i: (layer, 0, 0)),
            mod_spec(scale_idx),
            mod_spec(shift_idx),
        ],
        out_specs=pl.BlockSpec((tm, d), lambda i: (i, 0)),
        out_shape=jax.ShapeDtypeStruct((m, d), BF16),
        compiler_params=_params("parallel"),
        name="norm_mod",
    )(x, gain, mod, mod)


def _mm_kernel(x_ref, w_ref, o_ref):
    o_ref[...] = _dot(x_ref[...], w_ref[...]).astype(o_ref.dtype)


def _matmul(x, w, layer, *, tm, tn, out_dtype, col0=0):
    m, k = x.shape
    n = w.shape[-1] - col0
    col_block0 = col0 // tn
    return pl.pallas_call(
        _mm_kernel,
        grid=(m // tm, n // tn),
        in_specs=[
            pl.BlockSpec((tm, k), lambda i, j: (i, 0)),
            pl.BlockSpec((None, k, tn), lambda i, j: (layer, 0, col_block0 + j)),
        ],
        out_specs=pl.BlockSpec((tm, tn), lambda i, j: (i, j)),
        out_shape=jax.ShapeDtypeStruct((m, n), out_dtype),
        compiler_params=_params("parallel", "arbitrary"),
        name="matmul",
    )(x, w)


def _mm_resid_kernel(x_ref, w_ref, r_ref, gt_ref, o_ref):
    o_ref[...] = r_ref[...] + gt_ref[...] * _dot(x_ref[...], w_ref[...])


def _mm_resid_ksplit_kernel(x_ref, w_ref, r_ref, gt_ref, o_ref):
    @pl.when(pl.program_id(2) == 0)
    def _():
        o_ref[...] = r_ref[...] + gt_ref[...] * _dot(x_ref[...], w_ref[...])

    @pl.when(pl.program_id(2) > 0)
    def _():
        o_ref[...] += gt_ref[...] * _dot(x_ref[...], w_ref[...])


def _matmul_resid(x, w, layer, resid, mod, mod_layer, gate_idx, batch, seq, *, tm, tn, tk=None):
    m, k = x.shape
    n = w.shape[-1]
    per_batch = seq // tm
    gate_blocks = n // tn
    if tk is None:
        return pl.pallas_call(
            _mm_resid_kernel,
            grid=(m // tm, n // tn),
            in_specs=[
                pl.BlockSpec((tm, k), lambda i, j: (i, 0)),
                pl.BlockSpec((None, k, tn), lambda i, j: (layer, 0, j)),
                pl.BlockSpec((tm, tn), lambda i, j: (i, j)),
                pl.BlockSpec((None, 1, tn), lambda i, j: (mod_layer * batch + i // per_batch, 0,
                                                          gate_idx * gate_blocks + j)),
            ],
            out_specs=pl.BlockSpec((tm, tn), lambda i, j: (i, j)),
            out_shape=jax.ShapeDtypeStruct((m, n), F32),
            compiler_params=_params("parallel", "arbitrary"),
            name="matmul_resid",
        )(x, w, resid, mod)
    return pl.pallas_call(
        _mm_resid_ksplit_kernel,
        grid=(m // tm, n // tn, k // tk),
        in_specs=[
            pl.BlockSpec((tm, tk), lambda i, j, kk: (i, kk)),
            pl.BlockSpec((None, tk, tn), lambda i, j, kk: (layer, kk, j)),
            pl.BlockSpec((tm, tn), lambda i, j, kk: (i, j)),
            pl.BlockSpec((None, 1, tn), lambda i, j, kk: (mod_layer * batch + i // per_batch, 0,
                                                          gate_idx * gate_blocks + j)),
        ],
        out_specs=pl.BlockSpec((tm, tn), lambda i, j, kk: (i, j)),
        out_shape=jax.ShapeDtypeStruct((m, n), F32),
        compiler_params=_params("parallel", "parallel", "arbitrary"),
        name="matmul_resid_ksplit",
    )(x, w, resid, mod)


def _swiglu_kernel(x_ref, wg_ref, wu_ref, o_ref):
    x = x_ref[...]
    g = _dot(x, wg_ref[...])
    u = _dot(x, wu_ref[...])
    o_ref[...] = (g * jax.nn.sigmoid(g) * u).astype(o_ref.dtype)


def _swiglu_up(x, wg, wu, layer, *, tm, tn):
    m, k = x.shape
    n = wg.shape[-1]
    w_spec = pl.BlockSpec((None, k, tn), lambda i, j: (layer, 0, j))
    return pl.pallas_call(
        _swiglu_kernel,
        grid=(m // tm, n // tn),
        in_specs=[pl.BlockSpec((tm, k), lambda i, j: (i, 0)), w_spec, w_spec],
        out_specs=pl.BlockSpec((tm, tn), lambda i, j: (i, j)),
        out_shape=jax.ShapeDtypeStruct((m, n), BF16),
        compiler_params=_params("parallel", "arbitrary"),
        name="swiglu_up",
    )(x, wg, wu)


def _rope_mla(x, cos, s1, s2):
    half = MLA_ROPE_DIM // 2
    return (x * cos + pltpu.roll(x, LANES - half, 1) * s1 + pltpu.roll(x, half, 1) * s2)


def _rope_dil(x, cos, sin_signed):
    return x * cos + pltpu.roll(x, DIL_HEAD_DIM // 2, 1) * sin_signed


def _deferred_schedule(n_i, n_j):
    steps = n_i * n_j

    def cur(t):
        t = jnp.minimum(t, steps - 1)
        return t // n_j, t % n_j

    def prev(t):
        t = jnp.maximum(t - 1, 0)
        return t // n_j, t % n_j

    return steps, cur, prev


def _deferred_prologue(n_j, acc_ref):
    t = pl.program_id(0)

    @pl.when(t == 0)
    def _():
        acc_ref[...] = jnp.zeros_like(acc_ref)

    return lax.rem(t, n_j) == 0


def _mla_q_nope_kernel(n_j, cq_ref, gqa_ref, w_ref, gn_ref, o_ref, hs_ref, acc_ref):
    @pl.when(_deferred_prologue(n_j, acc_ref))
    def _():
        hs_ref[...] = _rms(cq_ref[...], gqa_ref[...]).astype(hs_ref.dtype)

    for t in range(acc_ref.shape[1] // LANES):
        c0 = t * LANES
        o_ref[:, c0:c0 + LANES] = _rms(acc_ref[:, c0:c0 + LANES], gn_ref[...]).astype(o_ref.dtype)
    acc_ref[...] = _dot(hs_ref[...], w_ref[...])


def _mla_q_pe_kernel(n_j, cq_ref, gqa_ref, w_ref, gp_ref, cos_ref, s1_ref, s2_ref, o_ref, hs_ref, acc_ref):
    @pl.when(_deferred_prologue(n_j, acc_ref))
    def _():
        hs_ref[...] = _rms(cq_ref[...], gqa_ref[...]).astype(hs_ref.dtype)

    cos, s1, s2 = cos_ref[...], s1_ref[...], s2_ref[...]
    low = lax.broadcasted_iota(jnp.int32, (1, LANES), 1) < MLA_ROPE_DIM
    for t in range(acc_ref.shape[1] // LANES):
        c0 = t * LANES
        x = acc_ref[:, c0:c0 + LANES]
        sq = x * x
        ss_low = jnp.sum(jnp.where(low, sq, 0.0), axis=-1, keepdims=True)
        ss_high = jnp.sum(jnp.where(low, 0.0, sq), axis=-1, keepdims=True)
        ms = jnp.where(low, ss_low, ss_high) * (1.0 / MLA_ROPE_DIM)
        y = x * lax.rsqrt(ms + NORM_EPS) * gp_ref[...]
        o_ref[:, c0:c0 + LANES] = _rope_mla(y, cos, s1, s2).astype(o_ref.dtype)
    acc_ref[...] = _dot(hs_ref[...], w_ref[...])


def _mla_q_proj(a, g_q_a, w, layer, g_nope, g_pe_pair, tables, *, tm, tn):
    m = a.shape[0]
    k = w.shape[1]
    n_nope = MLA_HEADS * MLA_NOPE_DIM
    n_pe = MLA_HEADS * MLA_ROPE_DIM
    cos, s1, s2 = tables
    gain = pl.BlockSpec((None, 1, LANES), lambda t: (layer, 0, 0))

    def call(body, name, n_out, col_block0, with_tables, extra_args):
        n_j = n_out // tn
        steps, cur, prev = _deferred_schedule(m // tm, n_j)
        tab = pl.BlockSpec((tm, LANES), lambda t: (prev(t)[0], 0))
        return pl.pallas_call(
            functools.partial(body, n_j),
            grid=(steps + 1,),
            in_specs=[
                pl.BlockSpec((tm, k), lambda t: (cur(t)[0], 0)),
                pl.BlockSpec((None, 1, k), lambda t: (layer, 0, 0)),
                pl.BlockSpec((None, k, tn), lambda t: (layer, 0, col_block0 + cur(t)[1])),
                gain,
            ] + ([tab, tab, tab] if with_tables else []),
            out_specs=pl.BlockSpec((tm, tn), lambda t: prev(t)),
            out_shape=jax.ShapeDtypeStruct((m, n_out), BF16),
            scratch_shapes=[pltpu.VMEM((tm, k), BF16), pltpu.VMEM((tm, tn), F32)],
            compiler_params=_params("arbitrary"),
            name=name,
        )(a, g_q_a, w, *extra_args)

    q_nope = call(_mla_q_nope_kernel, "mla_q_nope_proj", n_nope, 0, False, [g_nope])
    q_pe = call(_mla_q_pe_kernel, "mla_q_pe_proj", n_pe, n_nope // tn, True, [g_pe_pair, cos, s1, s2])
    return q_nope, q_pe


MLA_KV_BLOCK = 768


def _mla_kv_kernel(ckv_ref, gkva_ref, w_ref, gk_ref, gp_ref, cos_ref, s1_ref, s2_ref,
                   kv_ref, kpe_ref, hs_ref):
    @pl.when(pl.program_id(1) == 0)
    def _():
        blk = ckv_ref[...]
        hs_ref[...] = _rms(blk[:, :MLA_KV_RANK], gkva_ref[...]).astype(hs_ref.dtype)
        pe = _rms(blk[:, MLA_KV_RANK:MLA_KV_RANK + LANES], gp_ref[...], inv_n=1.0 / MLA_ROPE_DIM)
        pe = _rope_mla(pe, cos_ref[...], s1_ref[...], s2_ref[...])
        kpe_ref[:, :LANES] = pe.astype(kpe_ref.dtype)
        kpe_ref[:, LANES:] = pltpu.roll(pe, MLA_ROPE_DIM, 1).astype(kpe_ref.dtype)

    acc = _dot(hs_ref[...], w_ref[...])
    head_w = MLA_NOPE_DIM + MLA_V_DIM
    for hh in range(acc.shape[1] // head_w):
        c0 = hh * head_w
        kv_ref[:, c0:c0 + MLA_NOPE_DIM] = _rms(acc[:, c0:c0 + MLA_NOPE_DIM], gk_ref[...]).astype(kv_ref.dtype)
        kv_ref[:, c0 + MLA_NOPE_DIM:c0 + head_w] = acc[:, c0 + MLA_NOPE_DIM:c0 + head_w].astype(kv_ref.dtype)


def _mla_kv_proj(a, g_kv_a, w, layer, g_k_nope, g_pe_pad, tables, *, tm, tn):
    m = a.shape[0]
    k, n = w.shape[1:]
    cos, s1, s2 = tables
    tab = pl.BlockSpec((tm, LANES), lambda i, j: (i, 0))
    gain = pl.BlockSpec((None, 1, LANES), lambda i, j: (layer, 0, 0))
    kv_col_block = MLA_Q_RANK // MLA_KV_BLOCK
    return pl.pallas_call(
        _mla_kv_kernel,
        grid=(m // tm, n // tn),
        in_specs=[
            pl.BlockSpec((tm, MLA_KV_BLOCK), lambda i, j: (i, kv_col_block)),
            pl.BlockSpec((None, 1, k), lambda i, j: (layer, 0, 0)),
            pl.BlockSpec((None, k, tn), lambda i, j: (layer, 0, j)),
            gain, gain,
            tab, tab, tab,
        ],
        out_specs=[
            pl.BlockSpec((tm, tn), lambda i, j: (i, j)),
            pl.BlockSpec((tm, 2 * LANES), lambda i, j: (i, 0)),
        ],
        out_shape=[
            jax.ShapeDtypeStruct((m, n), BF16),
            jax.ShapeDtypeStruct((m, 2 * LANES), BF16),
        ],
        scratch_shapes=[pltpu.VMEM((tm, k), BF16)],
        compiler_params=_params("parallel", "arbitrary"),
        name="mla_kv_proj",
    )(a, g_kv_a, w, g_k_nope, g_pe_pad, cos, s1, s2)


def _softmax_pv(scores, values, scale):
    c = scale * LOG2_E
    m = None
    for s in scores:
        mi = jnp.max(s, axis=-1, keepdims=True)
        m = mi if m is None else jnp.maximum(m, mi)
    mc = m * c
    den = None
    acc = None
    for s, v in zip(scores, values):
        p = jnp.exp2(s * c - mc)
        li = jnp.sum(p, axis=-1, keepdims=True)
        ai = _dot(p.astype(v.dtype), v)
        den = li if den is None else den + li
        acc = ai if acc is None else acc + ai
    return acc / den


MLA_Q_BLOCK = 512


def _mla_attn_kernel(qn_ref, qpe_ref, kv_ref, kpe_ref, o_ref, kcat_ref):
    seq = qn_ref.shape[0]
    tq = MLA_Q_BLOCK
    scale = (MLA_NOPE_DIM + MLA_ROPE_DIM) ** -0.5
    head_w = MLA_NOPE_DIM + MLA_V_DIM
    for hh in range(2):
        kcat_ref[hh, :, :MLA_NOPE_DIM] = kv_ref[:, hh * head_w:hh * head_w + MLA_NOPE_DIM]
        kcat_ref[hh, :, MLA_NOPE_DIM:] = kpe_ref[:, hh * LANES:(hh + 1) * LANES]

    row = lax.broadcasted_iota(jnp.int32, (tq, tq), 0)
    col = lax.broadcasted_iota(jnp.int32, (tq, tq), 1)
    causal = col <= row
    for qb in range(seq // tq):
        lo = qb * tq
        qpe = qpe_ref[lo:lo + tq, :]
        for hh in range(2):
            q = jnp.concatenate([qn_ref[lo:lo + tq, hh * LANES:(hh + 1) * LANES], qpe], axis=1)
            v0 = hh * head_w + MLA_NOPE_DIM
            s_diag = jnp.where(causal, _dot_nt(q, kcat_ref[hh, lo:lo + tq, :]), -jnp.inf)
            scores = [s_diag]
            values = [kv_ref[lo:lo + tq, v0:v0 + MLA_V_DIM]]
            if qb > 0:
                scores.append(_dot_nt(q, kcat_ref[hh, 0:lo, :]))
                values.append(kv_ref[0:lo, v0:v0 + MLA_V_DIM])
            o = _softmax_pv(scores, values, scale)
            o_ref[lo:lo + tq, hh * MLA_V_DIM:(hh + 1) * MLA_V_DIM] = o.astype(o_ref.dtype)


def _mla_attention(q_nope, q_pe, kv, kpe, batch, seq):
    pairs = MLA_HEADS // 2
    return pl.pallas_call(
        _mla_attn_kernel,
        grid=(batch, pairs),
        in_specs=[
            pl.BlockSpec((seq, 2 * MLA_NOPE_DIM), lambda b, p: (b, p)),
            pl.BlockSpec((seq, LANES), lambda b, p: (b, p)),
            pl.BlockSpec((seq, 2 * (MLA_NOPE_DIM + MLA_V_DIM)), lambda b, p: (b, p)),
            pl.BlockSpec((seq, 2 * LANES), lambda b, p: (b, 0)),
        ],
        out_specs=pl.BlockSpec((seq, 2 * MLA_V_DIM), lambda b, p: (b, p)),
        out_shape=jax.ShapeDtypeStruct((batch * seq, MLA_HEADS * MLA_V_DIM), BF16),
        scratch_shapes=[pltpu.VMEM((2, seq, MLA_NOPE_DIM + LANES), BF16)],
        compiler_params=_params("parallel", "parallel"),
        name="mla_attention",
    )(q_nope, q_pe, kv, kpe)


def _dil_qk_kernel(n_j, x_ref, w_ref, gain_ref, cos_ref, sin_ref, o_ref, acc_ref):
    _deferred_prologue(n_j, acc_ref)
    cos, sin = cos_ref[...], sin_ref[...]
    for hh in range(acc_ref.shape[1] // DIL_HEAD_DIM):
        c0 = hh * DIL_HEAD_DIM
        y = _rms(acc_ref[:, c0:c0 + DIL_HEAD_DIM], gain_ref[:, c0:c0 + DIL_HEAD_DIM])
        o_ref[:, c0:c0 + DIL_HEAD_DIM] = _rope_dil(y, cos, sin).astype(o_ref.dtype)
    acc_ref[...] = _dot(x_ref[...], w_ref[...])


def _dil_qk_proj(x, w, layer, gain_cols, tables, *, tm, tn):
    m, k = x.shape
    n = gain_cols.shape[-1]
    cos, sin = tables
    n_j = n // tn
    steps, cur, prev = _deferred_schedule(m // tm, n_j)
    tab = pl.BlockSpec((tm, LANES), lambda t: (prev(t)[0], 0))
    return pl.pallas_call(
        functools.partial(_dil_qk_kernel, n_j),
        grid=(steps + 1,),
        in_specs=[
            pl.BlockSpec((tm, k), lambda t: (cur(t)[0], 0)),
            pl.BlockSpec((None, k, tn), lambda t: (layer, 0, cur(t)[1])),
            pl.BlockSpec((None, 1, tn), lambda t: (layer, 0, prev(t)[1])),
            tab, tab,
        ],
        out_specs=pl.BlockSpec((tm, tn), lambda t: prev(t)),
        out_shape=jax.ShapeDtypeStruct((m, n), BF16),
        scratch_shapes=[pltpu.VMEM((tm, tn), F32)],
        compiler_params=_params("arbitrary"),
        name="dil_qk_proj",
    )(x, w, gain_cols, cos, sin)


DIL_Q_BLOCK = 128


def _dil_bias(rows, cols, offset, window, dilation):
    dist = (offset + lax.broadcasted_iota(jnp.int32, (rows, cols), 0)
            - lax.broadcasted_iota(jnp.int32, (rows, cols), 1))
    valid = (dist >= 0) & (dist <= window) & ((dist & (dilation - 1)) == 0)
    return jnp.where(valid, 0.0, -jnp.inf).astype(F32)


def _dil_attn_kernel(*refs):
    n_groups = len(DIL_GROUPS)
    q_refs = refs[:n_groups]
    k_refs = refs[n_groups:2 * n_groups]
    v_refs = refs[2 * n_groups:3 * n_groups]
    o_ref = refs[3 * n_groups]
    seq = o_ref.shape[0]
    tq = DIL_Q_BLOCK
    scale = DIL_HEAD_DIM ** -0.5
    diag, full, cut = [], [], []
    for window, dilation in DIL_GROUPS:
        assert tq % dilation == 0 and window % tq == 0
        diag.append(_dil_bias(tq, tq, 0, window, dilation))
        full.append(_dil_bias(tq, window, window, window, dilation) if window < seq else None)
        n_cut = min(window, seq) - tq
        cut.append(_dil_bias(tq, n_cut, n_cut, seq, dilation) if n_cut > 0 and dilation > 1 else None)

    for sb in range(seq // tq):
        t0 = sb * tq
        scores, values = [], []
        for g, (window, dilation) in enumerate(DIL_GROUPS):
            q = q_refs[g][t0:t0 + tq, :]
            scores.append(_dot_nt(q, k_refs[g][t0:t0 + tq, :]) + diag[g])
            values.append(v_refs[g][t0:t0 + tq, :])
            n_prev = min(window, t0)
            if n_prev == 0:
                continue
            s_prev = _dot_nt(q, k_refs[g][t0 - n_prev:t0, :])
            if n_prev == window:
                s_prev = s_prev + full[g]
            elif cut[g] is not None:
                s_prev = s_prev + cut[g][:, cut[g].shape[1] - n_prev:]
            scores.append(s_prev)
            values.append(v_refs[g][t0 - n_prev:t0, :])
        o = _softmax_pv(scores, values, scale)
        o_ref[t0:t0 + tq, :] = o.astype(o_ref.dtype)


def _dil_attention(qk, v, batch, seq):
    n_groups = len(DIL_GROUPS)
    hd = DIL_HEAD_DIM

    def spec(part, g):
        c = (part * n_groups + g) * DIL_HEADS
        return pl.BlockSpec((seq, hd), lambda b, h: (b, c + h))

    return pl.pallas_call(
        _dil_attn_kernel,
        grid=(batch, DIL_HEADS),
        in_specs=[spec(part, g) for part in range(2) for g in range(n_groups)]
        + [spec(0, g) for g in range(n_groups)],
        out_specs=pl.BlockSpec((seq, hd), lambda b, h: (b, h)),
        out_shape=jax.ShapeDtypeStruct((batch * seq, DIL_HEADS * hd), BF16),
        compiler_params=_params("parallel", "parallel"),
        name="dil_attention",
    )(*([qk] * (2 * n_groups) + [v] * n_groups))


def _pad_last(x, width):
    return jnp.pad(x, [(0, 0)] * (x.ndim - 1) + [(0, width - x.shape[-1])])


def kernel(x, c, positions, w_cond, b_cond, w_mod, b_mod, g_mix_norm, g_ffn_norm, mla_w_in, mla_g_q_a, mla_g_kv_a, mla_w_q_b, mla_w_kv_b, mla_g_q_nope, mla_g_q_pe, mla_g_k_nope, mla_g_k_pe, mla_w_o, dil_w_qkv, dil_g_q, dil_g_k, dil_w_o, ffn_w_gate, ffn_w_up, ffn_w_down):
    batch, seq, d = x.shape
    depth = w_mod.shape[0]
    n_a = mla_w_in.shape[0]
    n_b = dil_w_qkv.shape[0]
    m = batch * seq
    hidden = ffn_w_gate.shape[-1]
    hidden_pad = -(-hidden // FFN_PAD_MULTIPLE) * FFN_PAD_MULTIPLE

    w_in = _pad_last(mla_w_in.astype(BF16), MLA_IN_PAD)
    qk_dim = MLA_NOPE_DIM + MLA_ROPE_DIM
    w_q_b = mla_w_q_b.astype(BF16).reshape(n_a, MLA_Q_RANK, MLA_HEADS, qk_dim)
    w_q_b = jnp.concatenate(
        [w_q_b[..., :MLA_NOPE_DIM].reshape(n_a, MLA_Q_RANK, MLA_HEADS * MLA_NOPE_DIM),
         w_q_b[..., MLA_NOPE_DIM:].reshape(n_a, MLA_Q_RANK, MLA_HEADS * MLA_ROPE_DIM)], axis=-1)
    w_kv_b = mla_w_kv_b.astype(BF16)
    w_o_a = mla_w_o.astype(BF16)
    w_qkv = dil_w_qkv.astype(BF16)
    w_o_b = dil_w_o.astype(BF16)
    w_gate = _cast_pad(ffn_w_gate, 2, hidden_pad)
    w_up = _cast_pad(ffn_w_up, 2, hidden_pad)
    w_down = _cast_pad(ffn_w_down, 1, hidden_pad)

    g_mix = g_mix_norm[:, None, :]
    g_ffn = g_ffn_norm[:, None, :]
    g_q_a = mla_g_q_a[:, None, :]
    g_kv_a = mla_g_kv_a[:, None, :]
    g_q_nope = mla_g_q_nope[:, None, :]
    g_k_nope = mla_g_k_nope[:, None, :]
    g_q_pe = jnp.concatenate([mla_g_q_pe, mla_g_q_pe], axis=-1)[:, None, :]
    g_k_pe = _pad_last(mla_g_k_pe, LANES)[:, None, :]
    n_groups = len(DIL_GROUPS)
    n_qk = 2 * n_groups * DIL_HEADS * DIL_HEAD_DIM
    dil_gain = jnp.stack([dil_g_q, dil_g_k], axis=1)
    dil_gain = jnp.broadcast_to(dil_gain[:, :, :, None, :], (n_b, 2, n_groups, DIL_HEADS, DIL_HEAD_DIM))
    dil_gain = dil_gain.reshape(n_b, 1, n_qk)

    mod = _cond_mod(c, w_cond, b_cond, w_mod, b_mod)
    cos_m, s1_m, s2_m, cos_d, sin_d = _rope_tables(positions)

    xr = x.reshape(m, d)
    for i in range(depth):
        j = i // 2
        h = _norm_mod(xr, g_mix, mod, i, 0, 1, batch, seq)
        if i % 2 == 0:
            a = _matmul(h, w_in, j, tm=1024, tn=MLA_KV_BLOCK, out_dtype=F32)
            q_nope, q_pe = _mla_q_proj(a, g_q_a, w_q_b, j, g_q_nope, g_q_pe, (cos_m, s1_m, s2_m),
                                       tm=1024, tn=1024)
            kv, kpe = _mla_kv_proj(a, g_kv_a, w_kv_b, j, g_k_nope, g_k_pe, (cos_m, s1_m, s2_m),
                                   tm=1024, tn=1024)
            o = _mla_attention(q_nope, q_pe, kv, kpe, batch, seq)
            xr = _matmul_resid(o, w_o_a, j, xr, mod, i, 2, batch, seq, tm=1024, tn=512)
        else:
            qk = _dil_qk_proj(h, w_qkv, j, dil_gain, (cos_d, sin_d), tm=1024, tn=1024)
            v = _matmul(h, w_qkv, j, tm=1024, tn=1024, out_dtype=BF16, col0=n_qk)
            o = _dil_attention(qk, v, batch, seq)
            xr = _matmul_resid(o, w_o_b, j, xr, mod, i, 2, batch, seq, tm=1024, tn=512)
        h = _norm_mod(xr, g_ffn, mod, i, 3, 4, batch, seq)
        act = _swiglu_up(h, w_gate, w_up, i, tm=1024, tn=512)
        xr = _matmul_resid(act, w_down, i, xr, mod, i, 5, batch, seq, tm=1024, tn=1024,
                           tk=hidden_pad // 4)
    return xr.reshape(batch, seq, d)
```

```python
import functools
import math

import jax
import jax.numpy as jnp
from jax import lax
from jax.experimental import pallas as pl
from jax.experimental.pallas import tpu as pltpu

F32 = jnp.float32
BF16 = jnp.bfloat16

ROPE_THETA = 10000.0
NORM_EPS = 1e-6
N_MOD = 6

MLA_HEADS = 32
MLA_Q_RANK = 1536
MLA_KV_RANK = 512
MLA_NOPE_DIM = 128
MLA_ROPE_DIM = 64
MLA_V_DIM = 128

DIL_GROUPS = ((128, 1), (512, 4), (2048, 16))
DIL_HEADS = 16
DIL_HEAD_DIM = 128

LANES = 128
V7X_VMEM_LIMIT_BYTES = 56 * 2**20

FFN_PAD_MULTIPLE = 1024
CAST_BLOCK = 256
MLA_IN_PAD = 2304
LOG2_E = math.log2(math.e)


def _params(*semantics):
    return pltpu.CompilerParams(dimension_semantics=semantics,
                                vmem_limit_bytes=V7X_VMEM_LIMIT_BYTES)


def _dot(a, b):
    return jnp.dot(a, b, preferred_element_type=F32)


def _dot_nt(a, b):
    return lax.dot_general(a, b, (((1,), (1,)), ((), ())), preferred_element_type=F32)


def _rms(x, gain, inv_n=None):
    if inv_n is None:
        ms = jnp.mean(x * x, axis=-1, keepdims=True)
    else:
        ms = jnp.sum(x * x, axis=-1, keepdims=True) * inv_n
    return x * lax.rsqrt(ms + NORM_EPS) * gain


def _cast_pad_kernel(axis, n_src_blocks, x_ref, o_ref):
    @pl.when(pl.program_id(axis) < n_src_blocks)
    def _():
        o_ref[...] = x_ref[...].astype(o_ref.dtype)

    @pl.when(pl.program_id(axis) >= n_src_blocks)
    def _():
        o_ref[...] = jnp.zeros_like(o_ref)


def _cast_pad(w, axis, padded):
    layers, rows, cols = w.shape
    n_src = w.shape[axis] // CAST_BLOCK
    n_dst = padded // CAST_BLOCK
    if axis == 2:
        block = (layers, rows, CAST_BLOCK)
        in_map = lambda j: (0, 0, jnp.minimum(j, n_src - 1))
        out_map = lambda j: (0, 0, j)
        out_shape = (layers, rows, padded)
    else:
        block = (layers, CAST_BLOCK, cols)
        in_map = lambda j: (0, jnp.minimum(j, n_src - 1), 0)
        out_map = lambda j: (0, j, 0)
        out_shape = (layers, padded, cols)
    return pl.pallas_call(
        functools.partial(_cast_pad_kernel, 0, n_src),
        grid=(n_dst,),
        in_specs=[pl.BlockSpec(block, in_map)],
        out_specs=pl.BlockSpec(block, out_map),
        out_shape=jax.ShapeDtypeStruct(out_shape, BF16),
        compiler_params=_params("arbitrary"),
        name="cast_pad",
    )(w)


def _cond_mod_kernel(c_ref, wc_ref, bc_ref, wm_ref, bm_ref, o_ref, e_ref):
    @pl.when((pl.program_id(0) == 0) & (pl.program_id(1) == 0))
    def _():
        z = _dot(c_ref[...].astype(BF16), wc_ref[...].astype(BF16)) + bc_ref[...]
        e_ref[...] = z * jax.nn.sigmoid(z)

    o_ref[...] = _dot(e_ref[...].astype(BF16), wm_ref[...].astype(BF16)) + bm_ref[...]


def _cond_mod(c, w_cond, b_cond, w_mod, b_mod):
    batch, d = c.shape
    depth, rank, n = w_mod.shape
    rows = 8
    tn = 2048
    c8 = jnp.pad(c, ((0, rows - batch), (0, 0)))
    out = pl.pallas_call(
        _cond_mod_kernel,
        grid=(depth, n // tn),
        in_specs=[
            pl.BlockSpec((rows, d), lambda l, j: (0, 0)),
            pl.BlockSpec((d, rank), lambda l, j: (0, 0)),
            pl.BlockSpec((1, rank), lambda l, j: (0, 0)),
            pl.BlockSpec((None, rank, tn), lambda l, j: (l, 0, j)),
            pl.BlockSpec((None, 1, tn), lambda l, j: (l, 0, j)),
        ],
        out_specs=pl.BlockSpec((None, rows, tn), lambda l, j: (l, 0, j)),
        out_shape=jax.ShapeDtypeStruct((depth, rows, n), F32),
        scratch_shapes=[pltpu.VMEM((rows, rank), F32)],
        compiler_params=_params("arbitrary", "arbitrary"),
        name="cond_mod",
    )(c8, w_cond, b_cond.reshape(1, rank), w_mod, b_mod.reshape(depth, 1, n))
    return out[:, :batch].reshape(depth * batch, 1, n)


def _rope_tables_kernel(pos_ref, cm_ref, s1_ref, s2_ref, cd_ref, sd_ref):
    pos = pos_ref[...].astype(F32)
    lane = lax.broadcasted_iota(jnp.int32, (1, LANES), 1)

    def angles(dim):
        half = dim // 2
        idx = (lane & (half - 1)).astype(F32)
        inv_freq = jnp.exp(idx * (-2.0 * math.log(ROPE_THETA) / dim))
        return pos * inv_freq

    ang = angles(MLA_ROPE_DIM)
    cos, sin = jnp.cos(ang), jnp.sin(ang)
    first_half = (lane & (MLA_ROPE_DIM - 1)) < MLA_ROPE_DIM // 2
    cm_ref[...] = cos
    s1_ref[...] = jnp.where(first_half, -sin, 0.0)
    s2_ref[...] = jnp.where(first_half, 0.0, sin)
    ang = angles(DIL_HEAD_DIM)
    cos, sin = jnp.cos(ang), jnp.sin(ang)
    cd_ref[...] = cos
    sd_ref[...] = jnp.where(lane < DIL_HEAD_DIM // 2, -sin, sin)


def _rope_tables(positions):
    batch, seq = positions.shape
    ts = 512
    m = batch * seq
    table = jax.ShapeDtypeStruct((m, LANES), F32)
    spec = pl.BlockSpec((ts, LANES), lambda i: (i, 0))
    return pl.pallas_call(
        _rope_tables_kernel,
        grid=(m // ts,),
        in_specs=[pl.BlockSpec((ts, 1), lambda i: (i, 0))],
        out_specs=[spec] * 5,
        out_shape=[table] * 5,
        compiler_params=_params("parallel"),
        name="rope_tables",
    )(positions.reshape(m, 1))


def _norm_mod_kernel(x_ref, g_ref, sc_ref, sh_ref, o_ref):
    y = _rms(x_ref[...], g_ref[...])
    o_ref[...] = (y * (1.0 + sc_ref[...]) + sh_ref[...]).astype(o_ref.dtype)


def _norm_mod(x, gain, mod, layer, shift_idx, scale_idx, batch, seq):
    m, d = x.shape
    tm = 512
    per_batch = seq // tm

    def mod_spec(which):
        return pl.BlockSpec((None, 1, d), lambda i: (layer * batch + i // per_batch, 0, which))

    return pl.pallas_call(
        _norm_mod_kernel,
        grid=(m // tm,),
        in_specs=[
            pl.BlockSpec((tm, d), lambda i: (i, 0)),
            pl.BlockSpec((None, 1, d), lambda i: (layer, 0, 0)),
            mod_spec(scale_idx),
            mod_spec(shift_idx),
        ],
        out_specs=pl.BlockSpec((tm, d), lambda i: (i, 0)),
        out_shape=jax.ShapeDtypeStruct((m, d), BF16),
        compiler_params=_params("parallel"),
        name="norm_mod",
    )(x, gain, mod, mod)


def _mm_kernel(x_ref, w_ref, o_ref):
    o_ref[...] = _dot(x_ref[...], w_ref[...].astype(BF16)).astype(o_ref.dtype)


def _row_block_spec(tm, k, index_map, single_buffer):
    if single_buffer:
        return pl.BlockSpec((tm, k), index_map, pipeline_mode=pl.Buffered(1))
    return pl.BlockSpec((tm, k), index_map)


def _matmul(x, w, layer, *, tm, tn, out_dtype, col0=0, single_buffer_x=False):
    m, k = x.shape
    n = w.shape[-1] - col0
    col_block0 = col0 // tn
    return pl.pallas_call(
        _mm_kernel,
        grid=(m // tm, n // tn),
        in_specs=[
            _row_block_spec(tm, k, lambda i, j: (i, 0), single_buffer_x),
            pl.BlockSpec((None, k, tn), lambda i, j: (layer, 0, col_block0 + j)),
        ],
        out_specs=pl.BlockSpec((tm, tn), lambda i, j: (i, j)),
        out_shape=jax.ShapeDtypeStruct((m, n), out_dtype),
        compiler_params=_params("parallel", "arbitrary"),
        name="matmul",
    )(x, w)


def _mm_resid_kernel(x_ref, w_ref, r_ref, gt_ref, o_ref):
    o_ref[...] = r_ref[...] + gt_ref[...] * _dot(x_ref[...], w_ref[...])


def _mm_resid_ksplit_kernel(x_ref, w_ref, r_ref, gt_ref, o_ref):
    @pl.when(pl.program_id(2) == 0)
    def _():
        o_ref[...] = r_ref[...] + gt_ref[...] * _dot(x_ref[...], w_ref[...])

    @pl.when(pl.program_id(2) > 0)
    def _():
        o_ref[...] += gt_ref[...] * _dot(x_ref[...], w_ref[...])


def _matmul_resid(x, w, layer, resid, mod, mod_layer, gate_idx, batch, seq, *, tm, tn, tk=None):
    m, k = x.shape
    n = w.shape[-1]
    per_batch = seq // tm
    gate_blocks = n // tn
    if tk is None:
        return pl.pallas_call(
            _mm_resid_kernel,
            grid=(m // tm, n // tn),
            in_specs=[
                pl.BlockSpec((tm, k), lambda i, j: (i, 0)),
                pl.BlockSpec((None, k, tn), lambda i, j: (layer, 0, j)),
                pl.BlockSpec((tm, tn), lambda i, j: (i, j)),
                pl.BlockSpec((None, 1, tn), lambda i, j: (mod_layer * batch + i // per_batch, 0,
                                                          gate_idx * gate_blocks + j)),
            ],
            out_specs=pl.BlockSpec((tm, tn), lambda i, j: (i, j)),
            out_shape=jax.ShapeDtypeStruct((m, n), F32),
            compiler_params=_params("parallel", "arbitrary"),
            name="matmul_resid",
        )(x, w, resid, mod)
    return pl.pallas_call(
        _mm_resid_ksplit_kernel,
        grid=(m // tm, n // tn, k // tk),
        in_specs=[
            pl.BlockSpec((tm, tk), lambda i, j, kk: (i, kk)),
            pl.BlockSpec((None, tk, tn), lambda i, j, kk: (layer, kk, j)),
            pl.BlockSpec((tm, tn), lambda i, j, kk: (i, j)),
            pl.BlockSpec((None, 1, tn), lambda i, j, kk: (mod_layer * batch + i // per_batch, 0,
                                                          gate_idx * gate_blocks + j)),
        ],
        out_specs=pl.BlockSpec((tm, tn), lambda i, j, kk: (i, j)),
        out_shape=jax.ShapeDtypeStruct((m, n), F32),
        compiler_params=_params("parallel", "parallel", "arbitrary"),
        name="matmul_resid_ksplit",
    )(x, w, resid, mod)


def _swiglu_kernel(n_src_blocks, x_ref, wg_ref, wu_ref, o_ref):
    @pl.when(pl.program_id(1) < n_src_blocks)
    def _():
        x = x_ref[...]
        g = _dot(x, wg_ref[...].astype(BF16))
        u = _dot(x, wu_ref[...].astype(BF16))
        o_ref[...] = (g * jax.nn.sigmoid(g) * u).astype(o_ref.dtype)

    @pl.when(pl.program_id(1) >= n_src_blocks)
    def _():
        o_ref[...] = jnp.zeros_like(o_ref)


def _swiglu_up(x, wg, wu, layer, n_out, *, tm, tn):
    m, k = x.shape
    n_src_blocks = wg.shape[-1] // tn
    w_spec = pl.BlockSpec((None, k, tn), lambda i, j: (layer, 0, jnp.minimum(j, n_src_blocks - 1)))
    return pl.pallas_call(
        functools.partial(_swiglu_kernel, n_src_blocks),
        grid=(m // tm, n_out // tn),
        in_specs=[_row_block_spec(tm, k, lambda i, j: (i, 0), True), w_spec, w_spec],
        out_specs=pl.BlockSpec((tm, tn), lambda i, j: (i, j)),
        out_shape=jax.ShapeDtypeStruct((m, n_out), BF16),
        compiler_params=_params("parallel", "arbitrary"),
        name="swiglu_up",
    )(x, wg, wu)


def _rope_mla(x, cos, s1, s2):
    half = MLA_ROPE_DIM // 2
    return (x * cos + pltpu.roll(x, LANES - half, 1) * s1 + pltpu.roll(x, half, 1) * s2)


def _rope_dil(x, cos, sin_signed):
    return x * cos + pltpu.roll(x, DIL_HEAD_DIM // 2, 1) * sin_signed


def _deferred_schedule(n_i, n_j):
    steps = n_i * n_j

    def cur(t):
        t = jnp.minimum(t, steps - 1)
        return t // n_j, t % n_j

    def prev(t):
        t = jnp.maximum(t - 1, 0)
        return t // n_j, t % n_j

    return steps, cur, prev


def _deferred_prologue(n_j, acc_ref):
    t = pl.program_id(0)

    @pl.when(t == 0)
    def _():
        acc_ref[...] = jnp.zeros_like(acc_ref)

    return lax.rem(t, n_j) == 0


def _mla_q_nope_kernel(n_j, cq_ref, gqa_ref, w_ref, gn_ref, o_ref, hs_ref, acc_ref):
    @pl.when(_deferred_prologue(n_j, acc_ref))
    def _():
        hs_ref[...] = _rms(cq_ref[...], gqa_ref[...]).astype(hs_ref.dtype)

    for t in range(acc_ref.shape[1] // LANES):
        c0 = t * LANES
        o_ref[:, c0:c0 + LANES] = _rms(acc_ref[:, c0:c0 + LANES], gn_ref[...]).astype(o_ref.dtype)
    acc_ref[...] = _dot(hs_ref[...], w_ref[...])


def _mla_q_pe_kernel(n_j, cq_ref, gqa_ref, w_ref, gp_ref, cos_ref, s1_ref, s2_ref, o_ref, hs_ref, acc_ref):
    @pl.when(_deferred_prologue(n_j, acc_ref))
    def _():
        hs_ref[...] = _rms(cq_ref[...], gqa_ref[...]).astype(hs_ref.dtype)

    cos, s1, s2 = cos_ref[...], s1_ref[...], s2_ref[...]
    low = lax.broadcasted_iota(jnp.int32, (1, LANES), 1) < MLA_ROPE_DIM
    for t in range(acc_ref.shape[1] // LANES):
        c0 = t * LANES
        x = acc_ref[:, c0:c0 + LANES]
        sq = x * x
        ss_low = jnp.sum(jnp.where(low, sq, 0.0), axis=-1, keepdims=True)
        ss_high = jnp.sum(jnp.where(low, 0.0, sq), axis=-1, keepdims=True)
        ms = jnp.where(low, ss_low, ss_high) * (1.0 / MLA_ROPE_DIM)
        y = x * lax.rsqrt(ms + NORM_EPS) * gp_ref[...]
        o_ref[:, c0:c0 + LANES] = _rope_mla(y, cos, s1, s2).astype(o_ref.dtype)
    acc_ref[...] = _dot(hs_ref[...], w_ref[...])


def _mla_q_proj(a, g_q_a, w, layer, g_nope, g_pe_pair, tables, *, tm, tn):
    m = a.shape[0]
    k = w.shape[1]
    n_nope = MLA_HEADS * MLA_NOPE_DIM
    n_pe = MLA_HEADS * MLA_ROPE_DIM
    cos, s1, s2 = tables
    gain = pl.BlockSpec((None, 1, LANES), lambda t: (layer, 0, 0))

    def call(body, name, n_out, col_block0, with_tables, extra_args):
        n_j = n_out // tn
        steps, cur, prev = _deferred_schedule(m // tm, n_j)
        tab = pl.BlockSpec((tm, LANES), lambda t: (prev(t)[0], 0))
        return pl.pallas_call(
            functools.partial(body, n_j),
            grid=(steps + 1,),
            in_specs=[
                pl.BlockSpec((tm, k), lambda t: (cur(t)[0], 0)),
                pl.BlockSpec((None, 1, k), lambda t: (layer, 0, 0)),
                pl.BlockSpec((None, k, tn), lambda t: (layer, 0, col_block0 + cur(t)[1])),
                gain,
            ] + ([tab, tab, tab] if with_tables else []),
            out_specs=pl.BlockSpec((tm, tn), lambda t: prev(t)),
            out_shape=jax.ShapeDtypeStruct((m, n_out), BF16),
            scratch_shapes=[pltpu.VMEM((tm, k), BF16), pltpu.VMEM((tm, tn), F32)],
            compiler_params=_params("arbitrary"),
            name=name,
        )(a, g_q_a, w, *extra_args)

    q_nope = call(_mla_q_nope_kernel, "mla_q_nope_proj", n_nope, 0, False, [g_nope])
    q_pe = call(_mla_q_pe_kernel, "mla_q_pe_proj", n_pe, n_nope // tn, True, [g_pe_pair, cos, s1, s2])
    return q_nope, q_pe


MLA_KV_BLOCK = 768


def _mla_kv_kernel(ckv_ref, gkva_ref, w_ref, gk_ref, gp_ref, cos_ref, s1_ref, s2_ref,
                   kv_ref, kpe_ref, hs_ref):
    @pl.when(pl.program_id(1) == 0)
    def _():
        blk = ckv_ref[...]
        hs_ref[...] = _rms(blk[:, :MLA_KV_RANK], gkva_ref[...]).astype(hs_ref.dtype)
        pe = _rms(blk[:, MLA_KV_RANK:MLA_KV_RANK + LANES], gp_ref[...], inv_n=1.0 / MLA_ROPE_DIM)
        pe = _rope_mla(pe, cos_ref[...], s1_ref[...], s2_ref[...])
        kpe_ref[:, :LANES] = pe.astype(kpe_ref.dtype)
        kpe_ref[:, LANES:] = pltpu.roll(pe, MLA_ROPE_DIM, 1).astype(kpe_ref.dtype)

    acc = _dot(hs_ref[...], w_ref[...])
    head_w = MLA_NOPE_DIM + MLA_V_DIM
    for hh in range(acc.shape[1] // head_w):
        c0 = hh * head_w
        kv_ref[:, c0:c0 + MLA_NOPE_DIM] = _rms(acc[:, c0:c0 + MLA_NOPE_DIM], gk_ref[...]).astype(kv_ref.dtype)
        kv_ref[:, c0 + MLA_NOPE_DIM:c0 + head_w] = acc[:, c0 + MLA_NOPE_DIM:c0 + head_w].astype(kv_ref.dtype)


def _mla_kv_proj(a, g_kv_a, w, layer, g_k_nope, g_pe_pad, tables, *, tm, tn):
    m = a.shape[0]
    k, n = w.shape[1:]
    cos, s1, s2 = tables
    tab = pl.BlockSpec((tm, LANES), lambda i, j: (i, 0))
    gain = pl.BlockSpec((None, 1, LANES), lambda i, j: (layer, 0, 0))
    kv_col_block = MLA_Q_RANK // MLA_KV_BLOCK
    return pl.pallas_call(
        _mla_kv_kernel,
        grid=(m // tm, n // tn),
        in_specs=[
            pl.BlockSpec((tm, MLA_KV_BLOCK), lambda i, j: (i, kv_col_block)),
            pl.BlockSpec((None, 1, k), lambda i, j: (layer, 0, 0)),
            pl.BlockSpec((None, k, tn), lambda i, j: (layer, 0, j)),
            gain, gain,
            tab, tab, tab,
        ],
        out_specs=[
            pl.BlockSpec((tm, tn), lambda i, j: (i, j)),
            pl.BlockSpec((tm, 2 * LANES), lambda i, j: (i, 0)),
        ],
        out_shape=[
            jax.ShapeDtypeStruct((m, n), BF16),
            jax.ShapeDtypeStruct((m, 2 * LANES), BF16),
        ],
        scratch_shapes=[pltpu.VMEM((tm, k), BF16)],
        compiler_params=_params("parallel", "arbitrary"),
        name="mla_kv_proj",
    )(a, g_kv_a, w, g_k_nope, g_pe_pad, cos, s1, s2)


def _softmax_pv(scores, values, scale):
    c = scale * LOG2_E
    m = None
    for s in scores:
        mi = jnp.max(s, axis=-1, keepdims=True)
        m = mi if m is None else jnp.maximum(m, mi)
    mc = m * c
    den = None
    acc = None
    for s, v in zip(scores, values):
        p = jnp.exp2(s * c - mc)
        li = jnp.sum(p, axis=-1, keepdims=True)
        ai = _dot(p.astype(v.dtype), v)
        den = li if den is None else den + li
        acc = ai if acc is None else acc + ai
    return acc / den


MLA_Q_BLOCK = 512


def _mla_attn_kernel(qn_ref, qpe_ref, kv_ref, kpe_ref, o_ref, kcat_ref):
    seq = qn_ref.shape[0]
    tq = MLA_Q_BLOCK
    scale = (MLA_NOPE_DIM + MLA_ROPE_DIM) ** -0.5
    head_w = MLA_NOPE_DIM + MLA_V_DIM
    for hh in range(2):
        kcat_ref[hh, :, :MLA_NOPE_DIM] = kv_ref[:, hh * head_w:hh * head_w + MLA_NOPE_DIM]
        kcat_ref[hh, :, MLA_NOPE_DIM:] = kpe_ref[:, hh * LANES:(hh + 1) * LANES]

    row = lax.broadcasted_iota(jnp.int32, (tq, tq), 0)
    col = lax.broadcasted_iota(jnp.int32, (tq, tq), 1)
    causal = col <= row
    for qb in range(seq // tq):
        lo = qb * tq
        qpe = qpe_ref[lo:lo + tq, :]
        for hh in range(2):
            q = jnp.concatenate([qn_ref[lo:lo + tq, hh * LANES:(hh + 1) * LANES], qpe], axis=1)
            v0 = hh * head_w + MLA_NOPE_DIM
            s_diag = jnp.where(causal, _dot_nt(q, kcat_ref[hh, lo:lo + tq, :]), -jnp.inf)
            scores = [s_diag]
            values = [kv_ref[lo:lo + tq, v0:v0 + MLA_V_DIM]]
            if qb > 0:
                scores.append(_dot_nt(q, kcat_ref[hh, 0:lo, :]))
                values.append(kv_ref[0:lo, v0:v0 + MLA_V_DIM])
            o = _softmax_pv(scores, values, scale)
            o_ref[lo:lo + tq, hh * MLA_V_DIM:(hh + 1) * MLA_V_DIM] = o.astype(o_ref.dtype)


def _mla_attention(q_nope, q_pe, kv, kpe, batch, seq):
    pairs = MLA_HEADS // 2
    return pl.pallas_call(
        _mla_attn_kernel,
        grid=(batch, pairs),
        in_specs=[
            pl.BlockSpec((seq, 2 * MLA_NOPE_DIM), lambda b, p: (b, p)),
            pl.BlockSpec((seq, LANES), lambda b, p: (b, p)),
            pl.BlockSpec((seq, 2 * (MLA_NOPE_DIM + MLA_V_DIM)), lambda b, p: (b, p)),
            pl.BlockSpec((seq, 2 * LANES), lambda b, p: (b, 0)),
        ],
        out_specs=pl.BlockSpec((seq, 2 * MLA_V_DIM), lambda b, p: (b, p)),
        out_shape=jax.ShapeDtypeStruct((batch * seq, MLA_HEADS * MLA_V_DIM), BF16),
        scratch_shapes=[pltpu.VMEM((2, seq, MLA_NOPE_DIM + LANES), BF16)],
        compiler_params=_params("parallel", "parallel"),
        name="mla_attention",
    )(q_nope, q_pe, kv, kpe)


def _dil_qk_kernel(n_j, x_ref, w_ref, gain_ref, cos_ref, sin_ref, o_ref, acc_ref):
    _deferred_prologue(n_j, acc_ref)
    cos, sin = cos_ref[...], sin_ref[...]
    for hh in range(acc_ref.shape[1] // DIL_HEAD_DIM):
        c0 = hh * DIL_HEAD_DIM
        y = _rms(acc_ref[:, c0:c0 + DIL_HEAD_DIM], gain_ref[:, c0:c0 + DIL_HEAD_DIM])
        o_ref[:, c0:c0 + DIL_HEAD_DIM] = _rope_dil(y, cos, sin).astype(o_ref.dtype)
    acc_ref[...] = _dot(x_ref[...], w_ref[...].astype(BF16))


def _dil_qk_proj(x, w, layer, gain_cols, tables, *, tm, tn):
    m, k = x.shape
    n = gain_cols.shape[-1]
    cos, sin = tables
    n_j = n // tn
    steps, cur, prev = _deferred_schedule(m // tm, n_j)
    tab = pl.BlockSpec((tm, LANES), lambda t: (prev(t)[0], 0))
    return pl.pallas_call(
        functools.partial(_dil_qk_kernel, n_j),
        grid=(steps + 1,),
        in_specs=[
            _row_block_spec(tm, k, lambda t: (cur(t)[0], 0), True),
            pl.BlockSpec((None, k, tn), lambda t: (layer, 0, cur(t)[1])),
            pl.BlockSpec((None, 1, tn), lambda t: (layer, 0, prev(t)[1])),
            tab, tab,
        ],
        out_specs=pl.BlockSpec((tm, tn), lambda t: prev(t)),
        out_shape=jax.ShapeDtypeStruct((m, n), BF16),
        scratch_shapes=[pltpu.VMEM((tm, tn), F32)],
        compiler_params=_params("arbitrary"),
        name="dil_qk_proj",
    )(x, w, gain_cols, cos, sin)


DIL_Q_BLOCK = 128


def _dil_bias(rows, cols, offset, window, dilation):
    dist = (offset + lax.broadcasted_iota(jnp.int32, (rows, cols), 0)
            - lax.broadcasted_iota(jnp.int32, (rows, cols), 1))
    valid = (dist >= 0) & (dist <= window) & ((dist & (dilation - 1)) == 0)
    return jnp.where(valid, 0.0, -jnp.inf).astype(F32)


def _dil_attn_kernel(*refs):
    n_groups = len(DIL_GROUPS)
    q_refs = refs[:n_groups]
    k_refs = refs[n_groups:2 * n_groups]
    v_refs = refs[2 * n_groups:3 * n_groups]
    o_ref = refs[3 * n_groups]
    seq = o_ref.shape[0]
    tq = DIL_Q_BLOCK
    scale = DIL_HEAD_DIM ** -0.5
    diag, full, cut = [], [], []
    for window, dilation in DIL_GROUPS:
        assert tq % dilation == 0 and window % tq == 0
        diag.append(_dil_bias(tq, tq, 0, window, dilation))
        full.append(_dil_bias(tq, window, window, window, dilation) if window < seq else None)
        n_cut = min(window, seq) - tq
        cut.append(_dil_bias(tq, n_cut, n_cut, seq, dilation) if n_cut > 0 and dilation > 1 else None)

    for sb in range(seq // tq):
        t0 = sb * tq
        scores, values = [], []
        for g, (window, dilation) in enumerate(DIL_GROUPS):
            q = q_refs[g][t0:t0 + tq, :]
            scores.append(_dot_nt(q, k_refs[g][t0:t0 + tq, :]) + diag[g])
            values.append(v_refs[g][t0:t0 + tq, :])
            n_prev = min(window, t0)
            if n_prev == 0:
                continue
            s_prev = _dot_nt(q, k_refs[g][t0 - n_prev:t0, :])
            if n_prev == window:
                s_prev = s_prev + full[g]
            elif cut[g] is not None:
                s_prev = s_prev + cut[g][:, cut[g].shape[1] - n_prev:]
            scores.append(s_prev)
            values.append(v_refs[g][t0 - n_prev:t0, :])
        o = _softmax_pv(scores, values, scale)
        o_ref[t0:t0 + tq, :] = o.astype(o_ref.dtype)


def _dil_attention(qk, v, batch, seq):
    n_groups = len(DIL_GROUPS)
    hd = DIL_HEAD_DIM

    def spec(part, g):
        c = (part * n_groups + g) * DIL_HEADS
        return pl.BlockSpec((seq, hd), lambda b, h: (b, c + h))

    return pl.pallas_call(
        _dil_attn_kernel,
        grid=(batch, DIL_HEADS),
        in_specs=[spec(part, g) for part in range(2) for g in range(n_groups)]
        + [spec(0, g) for g in range(n_groups)],
        out_specs=pl.BlockSpec((seq, hd), lambda b, h: (b, h)),
        out_shape=jax.ShapeDtypeStruct((batch * seq, DIL_HEADS * hd), BF16),
        compiler_params=_params("parallel", "parallel"),
        name="dil_attention",
    )(*([qk] * (2 * n_groups) + [v] * n_groups))


def _pad_last(x, width):
    return jnp.pad(x, [(0, 0)] * (x.ndim - 1) + [(0, width - x.shape[-1])])


def kernel(x, c, positions, w_cond, b_cond, w_mod, b_mod, g_mix_norm, g_ffn_norm, mla_w_in, mla_g_q_a, mla_g_kv_a, mla_w_q_b, mla_w_kv_b, mla_g_q_nope, mla_g_q_pe, mla_g_k_nope, mla_g_k_pe, mla_w_o, dil_w_qkv, dil_g_q, dil_g_k, dil_w_o, ffn_w_gate, ffn_w_up, ffn_w_down):
    batch, seq, d = x.shape
    depth = w_mod.shape[0]
    n_a = mla_w_in.shape[0]
    n_b = dil_w_qkv.shape[0]
    m = batch * seq
    hidden = ffn_w_gate.shape[-1]
    hidden_pad = -(-hidden // FFN_PAD_MULTIPLE) * FFN_PAD_MULTIPLE

    w_in = _pad_last(mla_w_in.astype(BF16), MLA_IN_PAD)
    qk_dim = MLA_NOPE_DIM + MLA_ROPE_DIM
    w_q_b = mla_w_q_b.astype(BF16).reshape(n_a, MLA_Q_RANK, MLA_HEADS, qk_dim)
    w_q_b = jnp.concatenate(
        [w_q_b[..., :MLA_NOPE_DIM].reshape(n_a, MLA_Q_RANK, MLA_HEADS * MLA_NOPE_DIM),
         w_q_b[..., MLA_NOPE_DIM:].reshape(n_a, MLA_Q_RANK, MLA_HEADS * MLA_ROPE_DIM)], axis=-1)
    w_kv_b = mla_w_kv_b.astype(BF16)
    w_o_a = mla_w_o.astype(BF16)
    w_o_b = dil_w_o.astype(BF16)
    w_down = _cast_pad(ffn_w_down, 1, hidden_pad)

    g_mix = g_mix_norm[:, None, :]
    g_ffn = g_ffn_norm[:, None, :]
    g_q_a = mla_g_q_a[:, None, :]
    g_kv_a = mla_g_kv_a[:, None, :]
    g_q_nope = mla_g_q_nope[:, None, :]
    g_k_nope = mla_g_k_nope[:, None, :]
    g_q_pe = jnp.concatenate([mla_g_q_pe, mla_g_q_pe], axis=-1)[:, None, :]
    g_k_pe = _pad_last(mla_g_k_pe, LANES)[:, None, :]
    n_groups = len(DIL_GROUPS)
    n_qk = 2 * n_groups * DIL_HEADS * DIL_HEAD_DIM
    dil_gain = jnp.stack([dil_g_q, dil_g_k], axis=1)
    dil_gain = jnp.broadcast_to(dil_gain[:, :, :, None, :], (n_b, 2, n_groups, DIL_HEADS, DIL_HEAD_DIM))
    dil_gain = dil_gain.reshape(n_b, 1, n_qk)

    mod = _cond_mod(c, w_cond, b_cond, w_mod, b_mod)
    cos_m, s1_m, s2_m, cos_d, sin_d = _rope_tables(positions)

    xr = x.reshape(m, d)
    for i in range(depth):
        j = i // 2
        h = _norm_mod(xr, g_mix, mod, i, 0, 1, batch, seq)
        if i % 2 == 0:
            a = _matmul(h, w_in, j, tm=1024, tn=MLA_KV_BLOCK, out_dtype=F32)
            q_nope, q_pe = _mla_q_proj(a, g_q_a, w_q_b, j, g_q_nope, g_q_pe, (cos_m, s1_m, s2_m),
                                       tm=1024, tn=1024)
            kv, kpe = _mla_kv_proj(a, g_kv_a, w_kv_b, j, g_k_nope, g_k_pe, (cos_m, s1_m, s2_m),
                                   tm=1024, tn=1024)
            o = _mla_attention(q_nope, q_pe, kv, kpe, batch, seq)
            xr = _matmul_resid(o, w_o_a, j, xr, mod, i, 2, batch, seq, tm=1024, tn=512)
        else:
            qk = _dil_qk_proj(h, dil_w_qkv, j, dil_gain, (cos_d, sin_d), tm=2048, tn=512)
            v = _matmul(h, dil_w_qkv, j, tm=2048, tn=512, out_dtype=BF16, col0=n_qk,
                        single_buffer_x=True)
            o = _dil_attention(qk, v, batch, seq)
            xr = _matmul_resid(o, w_o_b, j, xr, mod, i, 2, batch, seq, tm=1024, tn=512)
        h = _norm_mod(xr, g_ffn, mod, i, 3, 4, batch, seq)
        act = _swiglu_up(h, ffn_w_gate, ffn_w_up, i, hidden_pad, tm=2048, tn=CAST_BLOCK)
        xr = _matmul_resid(act, w_down, i, xr, mod, i, 5, batch, seq, tm=1024, tn=1024,
                           tk=hidden_pad // 4)
    return xr.reshape(batch, seq, d)
```

```python
import functools
import math

import jax
import jax.numpy as jnp
from jax import lax
from jax.experimental import pallas as pl
from jax.experimental.pallas import tpu as pltpu

F32 = jnp.float32
BF16 = jnp.bfloat16

ROPE_THETA = 10000.0
NORM_EPS = 1e-6
N_MOD = 6

MLA_HEADS = 32
MLA_Q_RANK = 1536
MLA_KV_RANK = 512
MLA_NOPE_DIM = 128
MLA_ROPE_DIM = 64
MLA_V_DIM = 128

DIL_GROUPS = ((128, 1), (512, 4), (2048, 16))
DIL_HEADS = 16
DIL_HEAD_DIM = 128

LANES = 128
V7X_VMEM_LIMIT_BYTES = 56 * 2**20

FFN_PAD_MULTIPLE = 1024
CAST_BLOCK = 256
MLA_IN_PAD = 2304
LOG2_E = math.log2(math.e)


def _params(*semantics):
    return pltpu.CompilerParams(dimension_semantics=semantics,
                                vmem_limit_bytes=V7X_VMEM_LIMIT_BYTES)


def _dot(a, b):
    return jnp.dot(a, b, preferred_element_type=F32)


def _dot_nt(a, b):
    return lax.dot_general(a, b, (((1,), (1,)), ((), ())), preferred_element_type=F32)


def _rms(x, gain, inv_n=None):
    if inv_n is None:
        ms = jnp.mean(x * x, axis=-1, keepdims=True)
    else:
        ms = jnp.sum(x * x, axis=-1, keepdims=True) * inv_n
    return x * lax.rsqrt(ms + NORM_EPS) * gain


def _cast_pad_kernel(axis, n_src_blocks, x_ref, o_ref):
    @pl.when(pl.program_id(axis) < n_src_blocks)
    def _():
        o_ref[...] = x_ref[...].astype(o_ref.dtype)

    @pl.when(pl.program_id(axis) >= n_src_blocks)
    def _():
        o_ref[...] = jnp.zeros_like(o_ref)


def _cast_pad(w, axis, padded):
    layers, rows, cols = w.shape
    n_src = w.shape[axis] // CAST_BLOCK
    n_dst = padded // CAST_BLOCK
    if axis == 2:
        block = (layers, rows, CAST_BLOCK)
        in_map = lambda j: (0, 0, jnp.minimum(j, n_src - 1))
        out_map = lambda j: (0, 0, j)
        out_shape = (layers, rows, padded)
    else:
        block = (layers, CAST_BLOCK, cols)
        in_map = lambda j: (0, jnp.minimum(j, n_src - 1), 0)
        out_map = lambda j: (0, j, 0)
        out_shape = (layers, padded, cols)
    return pl.pallas_call(
        functools.partial(_cast_pad_kernel, 0, n_src),
        grid=(n_dst,),
        in_specs=[pl.BlockSpec(block, in_map)],
        out_specs=pl.BlockSpec(block, out_map),
        out_shape=jax.ShapeDtypeStruct(out_shape, BF16),
        compiler_params=_params("arbitrary"),
        name="cast_pad",
    )(w)


def _cond_mod_kernel(c_ref, wc_ref, bc_ref, wm_ref, bm_ref, o_ref, e_ref):
    @pl.when((pl.program_id(0) == 0) & (pl.program_id(1) == 0))
    def _():
        z = _dot(c_ref[...].astype(BF16), wc_ref[...].astype(BF16)) + bc_ref[...]
        e_ref[...] = z * jax.nn.sigmoid(z)

    o_ref[...] = _dot(e_ref[...].astype(BF16), wm_ref[...].astype(BF16)) + bm_ref[...]


def _cond_mod(c, w_cond, b_cond, w_mod, b_mod):
    batch, d = c.shape
    depth, rank, n = w_mod.shape
    rows = 8
    tn = 2048
    c8 = jnp.pad(c, ((0, rows - batch), (0, 0)))
    out = pl.pallas_call(
        _cond_mod_kernel,
        grid=(depth, n // tn),
        in_specs=[
            pl.BlockSpec((rows, d), lambda l, j: (0, 0)),
            pl.BlockSpec((d, rank), lambda l, j: (0, 0)),
            pl.BlockSpec((1, rank), lambda l, j: (0, 0)),
            pl.BlockSpec((None, rank, tn), lambda l, j: (l, 0, j)),
            pl.BlockSpec((None, 1, tn), lambda l, j: (l, 0, j)),
        ],
        out_specs=pl.BlockSpec((None, rows, tn), lambda l, j: (l, 0, j)),
        out_shape=jax.ShapeDtypeStruct((depth, rows, n), F32),
        scratch_shapes=[pltpu.VMEM((rows, rank), F32)],
        compiler_params=_params("arbitrary", "arbitrary"),
        name="cond_mod",
    )(c8, w_cond, b_cond.reshape(1, rank), w_mod, b_mod.reshape(depth, 1, n))
    return out[:, :batch].reshape(depth * batch, 1, n)


def _rope_tables_kernel(pos_ref, cm_ref, s1_ref, s2_ref, cd_ref, sd_ref):
    pos = pos_ref[...].astype(F32)
    lane = lax.broadcasted_iota(jnp.int32, (1, LANES), 1)

    def angles(dim):
        half = dim // 2
        idx = (lane & (half - 1)).astype(F32)
        inv_freq = jnp.exp(idx * (-2.0 * math.log(ROPE_THETA) / dim))
        return pos * inv_freq

    ang = angles(MLA_ROPE_DIM)
    cos, sin = jnp.cos(ang), jnp.sin(ang)
    first_half = (lane & (MLA_ROPE_DIM - 1)) < MLA_ROPE_DIM // 2
    cm_ref[...] = cos
    s1_ref[...] = jnp.where(first_half, -sin, 0.0)
    s2_ref[...] = jnp.where(first_half, 0.0, sin)
    ang = angles(DIL_HEAD_DIM)
    cos, sin = jnp.cos(ang), jnp.sin(ang)
    cd_ref[...] = cos
    sd_ref[...] = jnp.where(lane < DIL_HEAD_DIM // 2, -sin, sin)


def _rope_tables(positions):
    batch, seq = positions.shape
    ts = 512
    m = batch * seq
    table = jax.ShapeDtypeStruct((m, LANES), F32)
    spec = pl.BlockSpec((ts, LANES), lambda i: (i, 0))
    return pl.pallas_call(
        _rope_tables_kernel,
        grid=(m // ts,),
        in_specs=[pl.BlockSpec((ts, 1), lambda i: (i, 0))],
        out_specs=[spec] * 5,
        out_shape=[table] * 5,
        compiler_params=_params("parallel"),
        name="rope_tables",
    )(positions.reshape(m, 1))


def _norm_mod_kernel(x_ref, g_ref, sc_ref, sh_ref, o_ref):
    y = _rms(x_ref[...], g_ref[...])
    o_ref[...] = (y * (1.0 + sc_ref[...]) + sh_ref[...]).astype(o_ref.dtype)


def _norm_mod(x, gain, mod, layer, shift_idx, scale_idx, batch, seq):
    m, d = x.shape
    tm = 512
    per_batch = seq // tm

    def mod_spec(which):
        return pl.BlockSpec((None, 1, d), lambda i: (layer * batch + i // per_batch, 0, which))

    return pl.pallas_call(
        _norm_mod_kernel,
        grid=(m // tm,),
        in_specs=[
            pl.BlockSpec((tm, d), lambda i: (i, 0)),
            pl.BlockSpec((None, 1, d), lambda i: (layer, 0, 0)),
            mod_spec(scale_idx),
            mod_spec(shift_idx),
        ],
        out_specs=pl.BlockSpec((tm, d), lambda i: (i, 0)),
        out_shape=jax.ShapeDtypeStruct((m, d), BF16),
        compiler_params=_params("parallel"),
        name="norm_mod",
    )(x, gain, mod, mod)


def _mm_kernel(x_ref, w_ref, o_ref):
    o_ref[...] = _dot(x_ref[...], w_ref[...].astype(BF16)).astype(o_ref.dtype)


def _row_block_spec(tm, k, index_map, single_buffer):
    if single_buffer:
        return pl.BlockSpec((tm, k), index_map, pipeline_mode=pl.Buffered(1))
    return pl.BlockSpec((tm, k), index_map)


def _matmul(x, w, layer, *, tm, tn, out_dtype, col0=0, single_buffer_x=False):
    m, k = x.shape
    n = w.shape[-1] - col0
    col_block0 = col0 // tn
    return pl.pallas_call(
        _mm_kernel,
        grid=(m // tm, n // tn),
        in_specs=[
            _row_block_spec(tm, k, lambda i, j: (i, 0), single_buffer_x),
            pl.BlockSpec((None, k, tn), lambda i, j: (layer, 0, col_block0 + j)),
        ],
        out_specs=pl.BlockSpec((tm, tn), lambda i, j: (i, j)),
        out_shape=jax.ShapeDtypeStruct((m, n), out_dtype),
        compiler_params=_params("parallel", "arbitrary"),
        name="matmul",
    )(x, w)


def _mm_resid_kernel(x_ref, w_ref, r_ref, gt_ref, o_ref):
    o_ref[...] = r_ref[...] + gt_ref[...] * _dot(x_ref[...], w_ref[...])


def _mm_resid_ksplit_kernel(x_ref, w_ref, r_ref, gt_ref, o_ref):
    @pl.when(pl.program_id(2) == 0)
    def _():
        o_ref[...] = r_ref[...] + gt_ref[...] * _dot(x_ref[...], w_ref[...])

    @pl.when(pl.program_id(2) > 0)
    def _():
        o_ref[...] += gt_ref[...] * _dot(x_ref[...], w_ref[...])


def _matmul_resid(x, w, layer, resid, mod, mod_layer, gate_idx, batch, seq, *, tm, tn, tk=None):
    m, k = x.shape
    n = w.shape[-1]
    per_batch = seq // tm
    gate_blocks = n // tn
    if tk is None:
        return pl.pallas_call(
            _mm_resid_kernel,
            grid=(m // tm, n // tn),
            in_specs=[
                pl.BlockSpec((tm, k), lambda i, j: (i, 0)),
                pl.BlockSpec((None, k, tn), lambda i, j: (layer, 0, j)),
                pl.BlockSpec((tm, tn), lambda i, j: (i, j)),
                pl.BlockSpec((None, 1, tn), lambda i, j: (mod_layer * batch + i // per_batch, 0,
                                                          gate_idx * gate_blocks + j)),
            ],
            out_specs=pl.BlockSpec((tm, tn), lambda i, j: (i, j)),
            out_shape=jax.ShapeDtypeStruct((m, n), F32),
            compiler_params=_params("parallel", "arbitrary"),
            name="matmul_resid",
        )(x, w, resid, mod)
    return pl.pallas_call(
        _mm_resid_ksplit_kernel,
        grid=(m // tm, n // tn, k // tk),
        in_specs=[
            pl.BlockSpec((tm, tk), lambda i, j, kk: (i, kk)),
            pl.BlockSpec((None, tk, tn), lambda i, j, kk: (layer, kk, j)),
            pl.BlockSpec((tm, tn), lambda i, j, kk: (i, j)),
            pl.BlockSpec((None, 1, tn), lambda i, j, kk: (mod_layer * batch + i // per_batch, 0,
                                                          gate_idx * gate_blocks + j)),
        ],
        out_specs=pl.BlockSpec((tm, tn), lambda i, j, kk: (i, j)),
        out_shape=jax.ShapeDtypeStruct((m, n), F32),
        compiler_params=_params("parallel", "parallel", "arbitrary"),
        name="matmul_resid_ksplit",
    )(x, w, resid, mod)


def _swiglu_kernel(n_src_blocks, x_ref, wg_ref, wu_ref, o_ref):
    @pl.when(pl.program_id(1) < n_src_blocks)
    def _():
        x = x_ref[...]
        g = _dot(x, wg_ref[...].astype(BF16))
        u = _dot(x, wu_ref[...].astype(BF16))
        o_ref[...] = (g * jax.nn.sigmoid(g) * u).astype(o_ref.dtype)

    @pl.when(pl.program_id(1) >= n_src_blocks)
    def _():
        o_ref[...] = jnp.zeros_like(o_ref)


def _swiglu_up(x, wg, wu, layer, n_out, *, tm, tn):
    m, k = x.shape
    n_src_blocks = wg.shape[-1] // tn
    w_spec = pl.BlockSpec((None, k, tn), lambda i, j: (layer, 0, jnp.minimum(j, n_src_blocks - 1)))
    return pl.pallas_call(
        functools.partial(_swiglu_kernel, n_src_blocks),
        grid=(m // tm, n_out // tn),
        in_specs=[_row_block_spec(tm, k, lambda i, j: (i, 0), True), w_spec, w_spec],
        out_specs=pl.BlockSpec((tm, tn), lambda i, j: (i, j)),
        out_shape=jax.ShapeDtypeStruct((m, n_out), BF16),
        compiler_params=_params("parallel", "arbitrary"),
        name="swiglu_up",
    )(x, wg, wu)


def _rope_mla(x, cos, s1, s2):
    half = MLA_ROPE_DIM // 2
    return (x * cos + pltpu.roll(x, LANES - half, 1) * s1 + pltpu.roll(x, half, 1) * s2)


def _rope_dil(x, cos, sin_signed):
    return x * cos + pltpu.roll(x, DIL_HEAD_DIM // 2, 1) * sin_signed


def _deferred_schedule(n_i, n_j):
    steps = n_i * n_j

    def cur(t):
        t = jnp.minimum(t, steps - 1)
        return t // n_j, t % n_j

    def prev(t):
        t = jnp.maximum(t - 1, 0)
        return t // n_j, t % n_j

    return steps, cur, prev


def _deferred_prologue(n_j, acc_ref):
    t = pl.program_id(0)

    @pl.when(t == 0)
    def _():
        acc_ref[...] = jnp.zeros_like(acc_ref)

    return lax.rem(t, n_j) == 0


def _mla_q_nope_kernel(n_j, cq_ref, gqa_ref, w_ref, gn_ref, o_ref, hs_ref, acc_ref):
    @pl.when(_deferred_prologue(n_j, acc_ref))
    def _():
        hs_ref[...] = _rms(cq_ref[...], gqa_ref[...]).astype(hs_ref.dtype)

    for t in range(acc_ref.shape[1] // LANES):
        c0 = t * LANES
        o_ref[:, c0:c0 + LANES] = _rms(acc_ref[:, c0:c0 + LANES], gn_ref[...]).astype(o_ref.dtype)
    acc_ref[...] = _dot(hs_ref[...], w_ref[...])


def _mla_q_pe_kernel(n_j, cq_ref, gqa_ref, w_ref, gp_ref, cos_ref, s1_ref, s2_ref, o_ref, hs_ref, acc_ref):
    @pl.when(_deferred_prologue(n_j, acc_ref))
    def _():
        hs_ref[...] = _rms(cq_ref[...], gqa_ref[...]).astype(hs_ref.dtype)

    cos, s1, s2 = cos_ref[...], s1_ref[...], s2_ref[...]
    low = lax.broadcasted_iota(jnp.int32, (1, LANES), 1) < MLA_ROPE_DIM
    for t in range(acc_ref.shape[1] // LANES):
        c0 = t * LANES
        x = acc_ref[:, c0:c0 + LANES]
        sq = x * x
        ss_low = jnp.sum(jnp.where(low, sq, 0.0), axis=-1, keepdims=True)
        ss_high = jnp.sum(jnp.where(low, 0.0, sq), axis=-1, keepdims=True)
        ms = jnp.where(low, ss_low, ss_high) * (1.0 / MLA_ROPE_DIM)
        y = x * lax.rsqrt(ms + NORM_EPS) * gp_ref[...]
        o_ref[:, c0:c0 + LANES] = _rope_mla(y, cos, s1, s2).astype(o_ref.dtype)
    acc_ref[...] = _dot(hs_ref[...], w_ref[...])


def _mla_q_proj(a, g_q_a, w, layer, g_nope, g_pe_pair, tables, *, tm, tn):
    m = a.shape[0]
    k = w.shape[1]
    n_nope = MLA_HEADS * MLA_NOPE_DIM
    n_pe = MLA_HEADS * MLA_ROPE_DIM
    cos, s1, s2 = tables
    gain = pl.BlockSpec((None, 1, LANES), lambda t: (layer, 0, 0))

    def call(body, name, n_out, col_block0, with_tables, extra_args):
        n_j = n_out // tn
        steps, cur, prev = _deferred_schedule(m // tm, n_j)
        tab = pl.BlockSpec((tm, LANES), lambda t: (prev(t)[0], 0))
        return pl.pallas_call(
            functools.partial(body, n_j),
            grid=(steps + 1,),
            in_specs=[
                pl.BlockSpec((tm, k), lambda t: (cur(t)[0], 0)),
                pl.BlockSpec((None, 1, k), lambda t: (layer, 0, 0)),
                pl.BlockSpec((None, k, tn), lambda t: (layer, 0, col_block0 + cur(t)[1])),
                gain,
            ] + ([tab, tab, tab] if with_tables else []),
            out_specs=pl.BlockSpec((tm, tn), lambda t: prev(t)),
            out_shape=jax.ShapeDtypeStruct((m, n_out), BF16),
            scratch_shapes=[pltpu.VMEM((tm, k), BF16), pltpu.VMEM((tm, tn), F32)],
            compiler_params=_params("arbitrary"),
            name=name,
        )(a, g_q_a, w, *extra_args)

    q_nope = call(_mla_q_nope_kernel, "mla_q_nope_proj", n_nope, 0, False, [g_nope])
    q_pe = call(_mla_q_pe_kernel, "mla_q_pe_proj", n_pe, n_nope // tn, True, [g_pe_pair, cos, s1, s2])
    return q_nope, q_pe


MLA_KV_BLOCK = 768


def _mla_kv_kernel(ckv_ref, gkva_ref, w_ref, gk_ref, gp_ref, cos_ref, s1_ref, s2_ref,
                   kv_ref, kpe_ref, hs_ref):
    @pl.when(pl.program_id(1) == 0)
    def _():
        blk = ckv_ref[...]
        hs_ref[...] = _rms(blk[:, :MLA_KV_RANK], gkva_ref[...]).astype(hs_ref.dtype)
        pe = _rms(blk[:, MLA_KV_RANK:MLA_KV_RANK + LANES], gp_ref[...], inv_n=1.0 / MLA_ROPE_DIM)
        pe = _rope_mla(pe, cos_ref[...], s1_ref[...], s2_ref[...])
        kpe_ref[:, :LANES] = pe.astype(kpe_ref.dtype)
        kpe_ref[:, LANES:] = pltpu.roll(pe, MLA_ROPE_DIM, 1).astype(kpe_ref.dtype)

    acc = _dot(hs_ref[...], w_ref[...])
    head_w = MLA_NOPE_DIM + MLA_V_DIM
    for hh in range(acc.shape[1] // head_w):
        c0 = hh * head_w
        kv_ref[:, c0:c0 + MLA_NOPE_DIM] = _rms(acc[:, c0:c0 + MLA_NOPE_DIM], gk_ref[...]).astype(kv_ref.dtype)
        kv_ref[:, c0 + MLA_NOPE_DIM:c0 + head_w] = acc[:, c0 + MLA_NOPE_DIM:c0 + head_w].astype(kv_ref.dtype)


def _mla_kv_proj(a, g_kv_a, w, layer, g_k_nope, g_pe_pad, tables, *, tm, tn):
    m = a.shape[0]
    k, n = w.shape[1:]
    cos, s1, s2 = tables
    tab = pl.BlockSpec((tm, LANES), lambda i, j: (i, 0))
    gain = pl.BlockSpec((None, 1, LANES), lambda i, j: (layer, 0, 0))
    kv_col_block = MLA_Q_RANK // MLA_KV_BLOCK
    return pl.pallas_call(
        _mla_kv_kernel,
        grid=(m // tm, n // tn),
        in_specs=[
            pl.BlockSpec((tm, MLA_KV_BLOCK), lambda i, j: (i, kv_col_block)),
            pl.BlockSpec((None, 1, k), lambda i, j: (layer, 0, 0)),
            pl.BlockSpec((None, k, tn), lambda i, j: (layer, 0, j)),
            gain, gain,
            tab, tab, tab,
        ],
        out_specs=[
            pl.BlockSpec((tm, tn), lambda i, j: (i, j)),
            pl.BlockSpec((tm, 2 * LANES), lambda i, j: (i, 0)),
        ],
        out_shape=[
            jax.ShapeDtypeStruct((m, n), BF16),
            jax.ShapeDtypeStruct((m, 2 * LANES), BF16),
        ],
        scratch_shapes=[pltpu.VMEM((tm, k), BF16)],
        compiler_params=_params("parallel", "arbitrary"),
        name="mla_kv_proj",
    )(a, g_kv_a, w, g_k_nope, g_pe_pad, cos, s1, s2)


def _softmax_pv(scores, values, scale):
    c = scale * LOG2_E
    m = None
    for s in scores:
        mi = jnp.max(s, axis=-1, keepdims=True)
        m = mi if m is None else jnp.maximum(m, mi)
    mc = m * c
    den = None
    acc = None
    for s, v in zip(scores, values):
        p = jnp.exp2(s * c - mc)
        li = jnp.sum(p, axis=-1, keepdims=True)
        ai = _dot(p.astype(v.dtype), v)
        den = li if den is None else den + li
        acc = ai if acc is None else acc + ai
    return acc / den


MLA_Q_BLOCK = 512


def _mla_attn_kernel(qn_ref, qpe_ref, kv_ref, kpe_ref, o_ref, kcat_ref):
    seq = qn_ref.shape[0]
    tq = MLA_Q_BLOCK
    scale = (MLA_NOPE_DIM + MLA_ROPE_DIM) ** -0.5
    head_w = MLA_NOPE_DIM + MLA_V_DIM
    for hh in range(2):
        kcat_ref[hh, :, :MLA_NOPE_DIM] = kv_ref[:, hh * head_w:hh * head_w + MLA_NOPE_DIM]
        kcat_ref[hh, :, MLA_NOPE_DIM:] = kpe_ref[:, hh * LANES:(hh + 1) * LANES]

    row = lax.broadcasted_iota(jnp.int32, (tq, tq), 0)
    col = lax.broadcasted_iota(jnp.int32, (tq, tq), 1)
    causal = col <= row
    def block_scores(qb, hh):
        lo = qb * tq
        q = jnp.concatenate([qn_ref[lo:lo + tq, hh * LANES:(hh + 1) * LANES], qpe_ref[lo:lo + tq, :]], axis=1)
        v0 = hh * head_w + MLA_NOPE_DIM
        scores = [jnp.where(causal, _dot_nt(q, kcat_ref[hh, lo:lo + tq, :]), -jnp.inf)]
        values = [kv_ref[lo:lo + tq, v0:v0 + MLA_V_DIM]]
        if qb > 0:
            scores.append(_dot_nt(q, kcat_ref[hh, 0:lo, :]))
            values.append(kv_ref[0:lo, v0:v0 + MLA_V_DIM])
        return scores, values

    order = [(qb, hh) for qb in range(seq // tq) for hh in range(2)]
    pending = block_scores(*order[0])
    for idx, (qb, hh) in enumerate(order):
        upcoming = block_scores(*order[idx + 1]) if idx + 1 < len(order) else None
        o = _softmax_pv(*pending, scale)
        o_ref[qb * tq:(qb + 1) * tq, hh * MLA_V_DIM:(hh + 1) * MLA_V_DIM] = o.astype(o_ref.dtype)
        pending = upcoming


def _mla_attention(q_nope, q_pe, kv, kpe, batch, seq):
    pairs = MLA_HEADS // 2
    return pl.pallas_call(
        _mla_attn_kernel,
        grid=(batch, pairs),
        in_specs=[
            pl.BlockSpec((seq, 2 * MLA_NOPE_DIM), lambda b, p: (b, p)),
            pl.BlockSpec((seq, LANES), lambda b, p: (b, p)),
            pl.BlockSpec((seq, 2 * (MLA_NOPE_DIM + MLA_V_DIM)), lambda b, p: (b, p)),
            pl.BlockSpec((seq, 2 * LANES), lambda b, p: (b, 0)),
        ],
        out_specs=pl.BlockSpec((seq, 2 * MLA_V_DIM), lambda b, p: (b, p)),
        out_shape=jax.ShapeDtypeStruct((batch * seq, MLA_HEADS * MLA_V_DIM), BF16),
        scratch_shapes=[pltpu.VMEM((2, seq, MLA_NOPE_DIM + LANES), BF16)],
        compiler_params=_params("parallel", "parallel"),
        name="mla_attention",
    )(q_nope, q_pe, kv, kpe)


def _dil_qk_kernel(n_j, x_ref, w_ref, gain_ref, cos_ref, sin_ref, o_ref, acc_ref):
    _deferred_prologue(n_j, acc_ref)
    cos, sin = cos_ref[...], sin_ref[...]
    for hh in range(acc_ref.shape[1] // DIL_HEAD_DIM):
        c0 = hh * DIL_HEAD_DIM
        y = _rms(acc_ref[:, c0:c0 + DIL_HEAD_DIM], gain_ref[:, c0:c0 + DIL_HEAD_DIM])
        o_ref[:, c0:c0 + DIL_HEAD_DIM] = _rope_dil(y, cos, sin).astype(o_ref.dtype)
    acc_ref[...] = _dot(x_ref[...], w_ref[...].astype(BF16))


def _dil_qk_proj(x, w, layer, gain_cols, tables, *, tm, tn):
    m, k = x.shape
    n = gain_cols.shape[-1]
    cos, sin = tables
    n_j = n // tn
    steps, cur, prev = _deferred_schedule(m // tm, n_j)
    tab = pl.BlockSpec((tm, LANES), lambda t: (prev(t)[0], 0))
    return pl.pallas_call(
        functools.partial(_dil_qk_kernel, n_j),
        grid=(steps + 1,),
        in_specs=[
            _row_block_spec(tm, k, lambda t: (cur(t)[0], 0), True),
            pl.BlockSpec((None, k, tn), lambda t: (layer, 0, cur(t)[1])),
            pl.BlockSpec((None, 1, tn), lambda t: (layer, 0, prev(t)[1])),
            tab, tab,
        ],
        out_specs=pl.BlockSpec((tm, tn), lambda t: prev(t)),
        out_shape=jax.ShapeDtypeStruct((m, n), BF16),
        scratch_shapes=[pltpu.VMEM((tm, tn), F32)],
        compiler_params=_params("arbitrary"),
        name="dil_qk_proj",
    )(x, w, gain_cols, cos, sin)


DIL_Q_BLOCK = 128


def _dil_bias(rows, cols, offset, window, dilation):
    dist = (offset + lax.broadcasted_iota(jnp.int32, (rows, cols), 0)
            - lax.broadcasted_iota(jnp.int32, (rows, cols), 1))
    valid = (dist >= 0) & (dist <= window) & ((dist & (dilation - 1)) == 0)
    return jnp.where(valid, 0.0, -jnp.inf).astype(F32)


def _dil_attn_kernel(*refs):
    n_groups = len(DIL_GROUPS)
    q_refs = refs[:n_groups]
    k_refs = refs[n_groups:2 * n_groups]
    v_refs = refs[2 * n_groups:3 * n_groups]
    o_ref = refs[3 * n_groups]
    seq = o_ref.shape[0]
    tq = DIL_Q_BLOCK
    scale = DIL_HEAD_DIM ** -0.5
    diag, full, cut = [], [], []
    for window, dilation in DIL_GROUPS:
        assert tq % dilation == 0 and window % tq == 0
        diag.append(_dil_bias(tq, tq, 0, window, dilation))
        full.append(_dil_bias(tq, window, window, window, dilation) if window < seq else None)
        n_cut = min(window, seq) - tq
        cut.append(_dil_bias(tq, n_cut, n_cut, seq, dilation) if n_cut > 0 and dilation > 1 else None)

    def block_scores(t0):
        scores, values = [], []
        for g, (window, dilation) in enumerate(DIL_GROUPS):
            q = q_refs[g][t0:t0 + tq, :]
            scores.append(_dot_nt(q, k_refs[g][t0:t0 + tq, :]) + diag[g])
            values.append(v_refs[g][t0:t0 + tq, :])
            n_prev = min(window, t0)
            if n_prev == 0:
                continue
            s_prev = _dot_nt(q, k_refs[g][t0 - n_prev:t0, :])
            if n_prev == window:
                s_prev = s_prev + full[g]
            elif cut[g] is not None:
                s_prev = s_prev + cut[g][:, cut[g].shape[1] - n_prev:]
            scores.append(s_prev)
            values.append(v_refs[g][t0 - n_prev:t0, :])
        return scores, values

    n_blocks = seq // tq
    pending = block_scores(0)
    for sb in range(n_blocks):
        upcoming = block_scores((sb + 1) * tq) if sb + 1 < n_blocks else None
        o = _softmax_pv(*pending, scale)
        o_ref[sb * tq:(sb + 1) * tq, :] = o.astype(o_ref.dtype)
        pending = upcoming


def _dil_attention(qk, v, batch, seq):
    n_groups = len(DIL_GROUPS)
    hd = DIL_HEAD_DIM

    def spec(part, g):
        c = (part * n_groups + g) * DIL_HEADS
        return pl.BlockSpec((seq, hd), lambda b, h: (b, c + h))

    return pl.pallas_call(
        _dil_attn_kernel,
        grid=(batch, DIL_HEADS),
        in_specs=[spec(part, g) for part in range(2) for g in range(n_groups)]
        + [spec(0, g) for g in range(n_groups)],
        out_specs=pl.BlockSpec((seq, hd), lambda b, h: (b, h)),
        out_shape=jax.ShapeDtypeStruct((batch * seq, DIL_HEADS * hd), BF16),
        compiler_params=_params("parallel", "parallel"),
        name="dil_attention",
    )(*([qk] * (2 * n_groups) + [v] * n_groups))


def _pad_last(x, width):
    return jnp.pad(x, [(0, 0)] * (x.ndim - 1) + [(0, width - x.shape[-1])])


def kernel(x, c, positions, w_cond, b_cond, w_mod, b_mod, g_mix_norm, g_ffn_norm, mla_w_in, mla_g_q_a, mla_g_kv_a, mla_w_q_b, mla_w_kv_b, mla_g_q_nope, mla_g_q_pe, mla_g_k_nope, mla_g_k_pe, mla_w_o, dil_w_qkv, dil_g_q, dil_g_k, dil_w_o, ffn_w_gate, ffn_w_up, ffn_w_down):
    batch, seq, d = x.shape
    depth = w_mod.shape[0]
    n_a = mla_w_in.shape[0]
    n_b = dil_w_qkv.shape[0]
    m = batch * seq
    hidden = ffn_w_gate.shape[-1]
    hidden_pad = -(-hidden // FFN_PAD_MULTIPLE) * FFN_PAD_MULTIPLE

    w_in = _pad_last(mla_w_in.astype(BF16), MLA_IN_PAD)
    qk_dim = MLA_NOPE_DIM + MLA_ROPE_DIM
    w_q_b = mla_w_q_b.astype(BF16).reshape(n_a, MLA_Q_RANK, MLA_HEADS, qk_dim)
    w_q_b = jnp.concatenate(
        [w_q_b[..., :MLA_NOPE_DIM].reshape(n_a, MLA_Q_RANK, MLA_HEADS * MLA_NOPE_DIM),
         w_q_b[..., MLA_NOPE_DIM:].reshape(n_a, MLA_Q_RANK, MLA_HEADS * MLA_ROPE_DIM)], axis=-1)
    w_kv_b = mla_w_kv_b.astype(BF16)
    w_o_a = mla_w_o.astype(BF16)
    w_o_b = dil_w_o.astype(BF16)
    w_down = _cast_pad(ffn_w_down, 1, hidden_pad)

    g_mix = g_mix_norm[:, None, :]
    g_ffn = g_ffn_norm[:, None, :]
    g_q_a = mla_g_q_a[:, None, :]
    g_kv_a = mla_g_kv_a[:, None, :]
    g_q_nope = mla_g_q_nope[:, None, :]
    g_k_nope = mla_g_k_nope[:, None, :]
    g_q_pe = jnp.concatenate([mla_g_q_pe, mla_g_q_pe], axis=-1)[:, None, :]
    g_k_pe = _pad_last(mla_g_k_pe, LANES)[:, None, :]
    n_groups = len(DIL_GROUPS)
    n_qk = 2 * n_groups * DIL_HEADS * DIL_HEAD_DIM
    dil_gain = jnp.stack([dil_g_q, dil_g_k], axis=1)
    dil_gain = jnp.broadcast_to(dil_gain[:, :, :, None, :], (n_b, 2, n_groups, DIL_HEADS, DIL_HEAD_DIM))
    dil_gain = dil_gain.reshape(n_b, 1, n_qk)

    mod = _cond_mod(c, w_cond, b_cond, w_mod, b_mod)
    cos_m, s1_m, s2_m, cos_d, sin_d = _rope_tables(positions)

    xr = x.reshape(m, d)
    for i in range(depth):
        j = i // 2
        h = _norm_mod(xr, g_mix, mod, i, 0, 1, batch, seq)
        if i % 2 == 0:
            a = _matmul(h, w_in, j, tm=1024, tn=MLA_KV_BLOCK, out_dtype=F32)
            q_nope, q_pe = _mla_q_proj(a, g_q_a, w_q_b, j, g_q_nope, g_q_pe, (cos_m, s1_m, s2_m),
                                       tm=1024, tn=1024)
            kv, kpe = _mla_kv_proj(a, g_kv_a, w_kv_b, j, g_k_nope, g_k_pe, (cos_m, s1_m, s2_m),
                                   tm=1024, tn=1024)
            o = _mla_attention(q_nope, q_pe, kv, kpe, batch, seq)
            xr = _matmul_resid(o, w_o_a, j, xr, mod, i, 2, batch, seq, tm=1024, tn=512)
        else:
            qk = _dil_qk_proj(h, dil_w_qkv, j, dil_gain, (cos_d, sin_d), tm=2048, tn=512)
            v = _matmul(h, dil_w_qkv, j, tm=2048, tn=512, out_dtype=BF16, col0=n_qk,
                        single_buffer_x=True)
            o = _dil_attention(qk, v, batch, seq)
            xr = _matmul_resid(o, w_o_b, j, xr, mod, i, 2, batch, seq, tm=1024, tn=512)
        h = _norm_mod(xr, g_ffn, mod, i, 3, 4, batch, seq)
        act = _swiglu_up(h, ffn_w_gate, ffn_w_up, i, hidden_pad, tm=2048, tn=CAST_BLOCK)
        xr = _matmul_resid(act, w_down, i, xr, mod, i, 5, batch, seq, tm=1024, tn=1024,
                           tk=hidden_pad // 4)
    return xr.reshape(batch, seq, d)
```

```python
import functools
import math

import jax
import jax.numpy as jnp
from jax import lax
from jax.experimental import pallas as pl
from jax.experimental.pallas import tpu as pltpu

F32 = jnp.float32
BF16 = jnp.bfloat16

ROPE_THETA = 10000.0
NORM_EPS = 1e-6
N_MOD = 6

MLA_HEADS = 32
MLA_Q_RANK = 1536
MLA_KV_RANK = 512
MLA_NOPE_DIM = 128
MLA_ROPE_DIM = 64
MLA_V_DIM = 128

DIL_GROUPS = ((128, 1), (512, 4), (2048, 16))
DIL_HEADS = 16
DIL_HEAD_DIM = 128

LANES = 128
V7X_VMEM_LIMIT_BYTES = 56 * 2**20

FFN_PAD_MULTIPLE = 1024
EPILOGUE_ROWS = 256
FFN_COL_BLOCK = 256
MLA_IN_PAD = 2304
LOG2_E = math.log2(math.e)


def _params(*semantics):
    return pltpu.CompilerParams(dimension_semantics=semantics,
                                vmem_limit_bytes=V7X_VMEM_LIMIT_BYTES)


def _dot(a, b):
    return jnp.dot(a, b, preferred_element_type=F32)


def _dot_nt(a, b):
    return lax.dot_general(a, b, (((1,), (1,)), ((), ())), preferred_element_type=F32)


def _rms(x, gain, inv_n=None):
    if inv_n is None:
        ms = jnp.mean(x * x, axis=-1, keepdims=True)
    else:
        ms = jnp.sum(x * x, axis=-1, keepdims=True) * inv_n
    return x * lax.rsqrt(ms + NORM_EPS) * gain


def _cond_mod_kernel(c_ref, wc_ref, bc_ref, wm_ref, bm_ref, o_ref, e_ref):
    @pl.when((pl.program_id(0) == 0) & (pl.program_id(1) == 0))
    def _():
        z = _dot(c_ref[...].astype(BF16), wc_ref[...].astype(BF16)) + bc_ref[...]
        e_ref[...] = z * jax.nn.sigmoid(z)

    o_ref[...] = _dot(e_ref[...].astype(BF16), wm_ref[...].astype(BF16)) + bm_ref[...]


def _cond_mod(c, w_cond, b_cond, w_mod, b_mod):
    batch, d = c.shape
    depth, rank, n = w_mod.shape
    rows = 8
    tn = 2048
    c8 = jnp.pad(c, ((0, rows - batch), (0, 0)))
    out = pl.pallas_call(
        _cond_mod_kernel,
        grid=(depth, n // tn),
        in_specs=[
            pl.BlockSpec((rows, d), lambda l, j: (0, 0)),
            pl.BlockSpec((d, rank), lambda l, j: (0, 0)),
            pl.BlockSpec((1, rank), lambda l, j: (0, 0)),
            pl.BlockSpec((None, rank, tn), lambda l, j: (l, 0, j)),
            pl.BlockSpec((None, 1, tn), lambda l, j: (l, 0, j)),
        ],
        out_specs=pl.BlockSpec((None, rows, tn), lambda l, j: (l, 0, j)),
        out_shape=jax.ShapeDtypeStruct((depth, rows, n), F32),
        scratch_shapes=[pltpu.VMEM((rows, rank), F32)],
        compiler_params=_params("arbitrary", "arbitrary"),
        name="cond_mod",
    )(c8, w_cond, b_cond.reshape(1, rank), w_mod, b_mod.reshape(depth, 1, n))
    return out[:, :batch].reshape(depth * batch, 1, n)


def _rope_tables_kernel(pos_ref, cm_ref, s1_ref, s2_ref, cd_ref, sd_ref):
    pos = pos_ref[...].astype(F32)
    lane = lax.broadcasted_iota(jnp.int32, (1, LANES), 1)

    def angles(dim):
        half = dim // 2
        idx = (lane & (half - 1)).astype(F32)
        inv_freq = jnp.exp(idx * (-2.0 * math.log(ROPE_THETA) / dim))
        return pos * inv_freq

    ang = angles(MLA_ROPE_DIM)
    cos, sin = jnp.cos(ang), jnp.sin(ang)
    first_half = (lane & (MLA_ROPE_DIM - 1)) < MLA_ROPE_DIM // 2
    cm_ref[...] = cos
    s1_ref[...] = jnp.where(first_half, -sin, 0.0)
    s2_ref[...] = jnp.where(first_half, 0.0, sin)
    ang = angles(DIL_HEAD_DIM)
    cos, sin = jnp.cos(ang), jnp.sin(ang)
    cd_ref[...] = cos
    sd_ref[...] = jnp.where(lane < DIL_HEAD_DIM // 2, -sin, sin)


def _rope_tables(positions):
    batch, seq = positions.shape
    ts = 512
    m = batch * seq
    table = jax.ShapeDtypeStruct((m, LANES), F32)
    spec = pl.BlockSpec((ts, LANES), lambda i: (i, 0))
    return pl.pallas_call(
        _rope_tables_kernel,
        grid=(m // ts,),
        in_specs=[pl.BlockSpec((ts, 1), lambda i: (i, 0))],
        out_specs=[spec] * 5,
        out_shape=[table] * 5,
        compiler_params=_params("parallel"),
        name="rope_tables",
    )(positions.reshape(m, 1))


def _norm_mod_kernel(x_ref, g_ref, sc_ref, sh_ref, o_ref):
    y = _rms(x_ref[...], g_ref[...])
    o_ref[...] = (y * (1.0 + sc_ref[...]) + sh_ref[...]).astype(o_ref.dtype)


def _norm_mod(x, gain, mod, layer, shift_idx, scale_idx, batch, seq):
    m, d = x.shape
    tm = 512
    per_batch = seq // tm

    def mod_spec(which):
        return pl.BlockSpec((None, 1, d), lambda i: (layer * batch + i // per_batch, 0, which))

    return pl.pallas_call(
        _norm_mod_kernel,
        grid=(m // tm,),
        in_specs=[
            pl.BlockSpec((tm, d), lambda i: (i, 0)),
            pl.BlockSpec((None, 1, d), lambda i: (layer, 0, 0)),
            mod_spec(scale_idx),
            mod_spec(shift_idx),
        ],
        out_specs=pl.BlockSpec((tm, d), lambda i: (i, 0)),
        out_shape=jax.ShapeDtypeStruct((m, d), BF16),
        compiler_params=_params("parallel"),
        name="norm_mod",
    )(x, gain, mod, mod)


def _mm_kernel(x_ref, w_ref, o_ref):
    o_ref[...] = _dot(x_ref[...], w_ref[...].astype(BF16)).astype(o_ref.dtype)


def _row_block_spec(tm, k, index_map, single_buffer):
    if single_buffer:
        return pl.BlockSpec((tm, k), index_map, pipeline_mode=pl.Buffered(1))
    return pl.BlockSpec((tm, k), index_map)


def _matmul(x, w, layer, *, tm, tn, out_dtype, col0=0, single_buffer_x=False):
    m, k = x.shape
    n = w.shape[-1] - col0
    col_block0 = col0 // tn
    return pl.pallas_call(
        _mm_kernel,
        grid=(m // tm, n // tn),
        in_specs=[
            _row_block_spec(tm, k, lambda i, j: (i, 0), single_buffer_x),
            pl.BlockSpec((None, k, tn), lambda i, j: (layer, 0, col_block0 + j)),
        ],
        out_specs=pl.BlockSpec((tm, tn), lambda i, j: (i, j)),
        out_shape=jax.ShapeDtypeStruct((m, n), out_dtype),
        compiler_params=_params("parallel", "arbitrary"),
        name="matmul",
    )(x, w)


def _mm_resid_kernel(x_ref, w_ref, r_ref, gt_ref, o_ref):
    o_ref[...] = r_ref[...] + gt_ref[...] * _dot(x_ref[...], w_ref[...])


def _mm_resid_ksplit_kernel(x_ref, w_ref, r_ref, gt_ref, o_ref):
    @pl.when(pl.program_id(2) == 0)
    def _():
        o_ref[...] = r_ref[...] + gt_ref[...] * _dot(x_ref[...], w_ref[...])

    @pl.when(pl.program_id(2) > 0)
    def _():
        o_ref[...] += gt_ref[...] * _dot(x_ref[...], w_ref[...])


def _matmul_resid(x, w, layer, resid, mod, mod_layer, gate_idx, batch, seq, *, tm, tn, tk=None):
    m, k = x.shape
    n = w.shape[-1]
    per_batch = seq // tm
    gate_blocks = n // tn
    if tk is None:
        return pl.pallas_call(
            _mm_resid_kernel,
            grid=(m // tm, n // tn),
            in_specs=[
                pl.BlockSpec((tm, k), lambda i, j: (i, 0)),
                pl.BlockSpec((None, k, tn), lambda i, j: (layer, 0, j)),
                pl.BlockSpec((tm, tn), lambda i, j: (i, j)),
                pl.BlockSpec((None, 1, tn), lambda i, j: (mod_layer * batch + i // per_batch, 0,
                                                          gate_idx * gate_blocks + j)),
            ],
            out_specs=pl.BlockSpec((tm, tn), lambda i, j: (i, j)),
            out_shape=jax.ShapeDtypeStruct((m, n), F32),
            compiler_params=_params("parallel", "arbitrary"),
            name="matmul_resid",
        )(x, w, resid, mod)
    return pl.pallas_call(
        _mm_resid_ksplit_kernel,
        grid=(m // tm, n // tn, k // tk),
        in_specs=[
            pl.BlockSpec((tm, tk), lambda i, j, kk: (i, kk)),
            pl.BlockSpec((None, tk, tn), lambda i, j, kk: (layer, kk, j)),
            pl.BlockSpec((tm, tn), lambda i, j, kk: (i, j)),
            pl.BlockSpec((None, 1, tn), lambda i, j, kk: (mod_layer * batch + i // per_batch, 0,
                                                          gate_idx * gate_blocks + j)),
        ],
        out_specs=pl.BlockSpec((tm, tn), lambda i, j, kk: (i, j)),
        out_shape=jax.ShapeDtypeStruct((m, n), F32),
        compiler_params=_params("parallel", "parallel", "arbitrary"),
        name="matmul_resid_ksplit",
    )(x, w, resid, mod)


def _swiglu_kernel(n_src_blocks, x_ref, wg_ref, wu_ref, wd_ref, o_ref, wd_out_ref):
    first_sweep = pl.program_id(0) == 0
    in_range = pl.program_id(1) < n_src_blocks

    @pl.when(in_range)
    def _():
        x = x_ref[...]
        g = _dot(x, wg_ref[...].astype(BF16))
        u = _dot(x, wu_ref[...].astype(BF16))
        o_ref[...] = (g * jax.nn.sigmoid(g) * u).astype(o_ref.dtype)

    @pl.when(jnp.logical_not(in_range))
    def _():
        o_ref[...] = jnp.zeros_like(o_ref)

    @pl.when(first_sweep & in_range)
    def _():
        wd_out_ref[...] = wd_ref[...].astype(wd_out_ref.dtype)

    @pl.when(first_sweep & jnp.logical_not(in_range))
    def _():
        wd_out_ref[...] = jnp.zeros_like(wd_out_ref)


def _swiglu_up(x, wg, wu, wd, layer, n_out, *, tm, tn):
    m, k = x.shape
    d = wd.shape[-1]
    n_src_blocks = wg.shape[-1] // tn
    n_dst_blocks = n_out // tn
    w_spec = pl.BlockSpec((None, k, tn), lambda i, j: (layer, 0, jnp.minimum(j, n_src_blocks - 1)))
    wd_in = pl.BlockSpec((None, tn, d), lambda i, j: (
        layer, jnp.where(i == 0, jnp.minimum(j, n_src_blocks - 1), n_src_blocks - 1), 0))
    wd_out = pl.BlockSpec((tn, d), lambda i, j: (jnp.where(i == 0, j, n_dst_blocks - 1), 0))
    return pl.pallas_call(
        functools.partial(_swiglu_kernel, n_src_blocks),
        grid=(m // tm, n_dst_blocks),
        in_specs=[_row_block_spec(tm, k, lambda i, j: (i, 0), True), w_spec, w_spec, wd_in],
        out_specs=[pl.BlockSpec((tm, tn), lambda i, j: (i, j)), wd_out],
        out_shape=[jax.ShapeDtypeStruct((m, n_out), BF16), jax.ShapeDtypeStruct((n_out, d), BF16)],
        compiler_params=_params("arbitrary", "arbitrary"),
        name="swiglu_up",
    )(x, wg, wu, wd)


def _rope_mla(x, cos, s1, s2):
    half = MLA_ROPE_DIM // 2
    return (x * cos + pltpu.roll(x, LANES - half, 1) * s1 + pltpu.roll(x, half, 1) * s2)


def _rope_dil(x, cos, sin_signed):
    return x * cos + pltpu.roll(x, DIL_HEAD_DIM // 2, 1) * sin_signed


def _deferred_schedule(n_i, n_j):
    steps = n_i * n_j

    def cur(t):
        t = jnp.minimum(t, steps - 1)
        return t // n_j, t % n_j

    def prev(t):
        t = jnp.maximum(t - 1, 0)
        return t // n_j, t % n_j

    return steps, cur, prev


def _deferred_prologue(n_j, acc_ref):
    t = pl.program_id(0)

    @pl.when(t == 0)
    def _():
        acc_ref[...] = jnp.zeros_like(acc_ref)

    return lax.rem(t, n_j) == 0


def _mla_q_nope_kernel(n_j, cq_ref, gqa_ref, w_ref, gn_ref, o_ref, hs_ref, acc_ref):
    @pl.when(_deferred_prologue(n_j, acc_ref))
    def _():
        hs_ref[...] = _rms(cq_ref[...], gqa_ref[...]).astype(hs_ref.dtype)

    for t in range(acc_ref.shape[1] // LANES):
        cols = slice(t * LANES, (t + 1) * LANES)
        o_ref[:, cols] = _rms(acc_ref[:, cols], gn_ref[...]).astype(o_ref.dtype)
    acc_ref[...] = _dot(hs_ref[...], w_ref[...])


def _mla_q_pe_kernel(n_j, cq_ref, gqa_ref, w_ref, gp_ref, cos_ref, s1_ref, s2_ref, o_ref, hs_ref, acc_ref):
    @pl.when(_deferred_prologue(n_j, acc_ref))
    def _():
        hs_ref[...] = _rms(cq_ref[...], gqa_ref[...]).astype(hs_ref.dtype)

    cos, s1, s2 = cos_ref[...], s1_ref[...], s2_ref[...]
    low = lax.broadcasted_iota(jnp.int32, (1, LANES), 1) < MLA_ROPE_DIM
    for t in range(acc_ref.shape[1] // LANES):
        cols = slice(t * LANES, (t + 1) * LANES)
        x = acc_ref[:, cols]
        sq = x * x
        ss_low = jnp.sum(jnp.where(low, sq, 0.0), axis=-1, keepdims=True)
        ss_high = jnp.sum(jnp.where(low, 0.0, sq), axis=-1, keepdims=True)
        ms = jnp.where(low, ss_low, ss_high) * (1.0 / MLA_ROPE_DIM)
        y = x * lax.rsqrt(ms + NORM_EPS) * gp_ref[...]
        o_ref[:, cols] = _rope_mla(y, cos, s1, s2).astype(o_ref.dtype)
    acc_ref[...] = _dot(hs_ref[...], w_ref[...])


def _mla_q_proj(a, g_q_a, w, layer, g_nope, g_pe_pair, tables, *, tm, tn):
    m = a.shape[0]
    k = w.shape[1]
    n_nope = MLA_HEADS * MLA_NOPE_DIM
    n_pe = MLA_HEADS * MLA_ROPE_DIM
    cos, s1, s2 = tables
    gain = pl.BlockSpec((None, 1, LANES), lambda t: (layer, 0, 0))

    def call(body, name, n_out, col_block0, with_tables, extra_args):
        n_j = n_out // tn
        steps, cur, prev = _deferred_schedule(m // tm, n_j)
        tab = pl.BlockSpec((tm, LANES), lambda t: (prev(t)[0], 0))
        return pl.pallas_call(
            functools.partial(body, n_j),
            grid=(steps + 1,),
            in_specs=[
                pl.BlockSpec((tm, k), lambda t: (cur(t)[0], 0)),
                pl.BlockSpec((None, 1, k), lambda t: (layer, 0, 0)),
                pl.BlockSpec((None, k, tn), lambda t: (layer, 0, col_block0 + cur(t)[1])),
                gain,
            ] + ([tab, tab, tab] if with_tables else []),
            out_specs=pl.BlockSpec((tm, tn), lambda t: prev(t)),
            out_shape=jax.ShapeDtypeStruct((m, n_out), BF16),
            scratch_shapes=[pltpu.VMEM((tm, k), BF16), pltpu.VMEM((tm, tn), F32)],
            compiler_params=_params("arbitrary"),
            name=name,
        )(a, g_q_a, w, *extra_args)

    q_nope = call(_mla_q_nope_kernel, "mla_q_nope_proj", n_nope, 0, False, [g_nope])
    q_pe = call(_mla_q_pe_kernel, "mla_q_pe_proj", n_pe, n_nope // tn, True, [g_pe_pair, cos, s1, s2])
    return q_nope, q_pe


MLA_KV_BLOCK = 768


def _mla_kv_kernel(ckv_ref, gkva_ref, w_ref, gk_ref, gp_ref, cos_ref, s1_ref, s2_ref,
                   kv_ref, kpe_ref, hs_ref):
    @pl.when(pl.program_id(1) == 0)
    def _():
        blk = ckv_ref[...]
        hs_ref[...] = _rms(blk[:, :MLA_KV_RANK], gkva_ref[...]).astype(hs_ref.dtype)
        pe = _rms(blk[:, MLA_KV_RANK:MLA_KV_RANK + LANES], gp_ref[...], inv_n=1.0 / MLA_ROPE_DIM)
        pe = _rope_mla(pe, cos_ref[...], s1_ref[...], s2_ref[...])
        kpe_ref[:, :LANES] = pe.astype(kpe_ref.dtype)
        kpe_ref[:, LANES:] = pltpu.roll(pe, MLA_ROPE_DIM, 1).astype(kpe_ref.dtype)

    acc = _dot(hs_ref[...], w_ref[...])
    head_w = MLA_NOPE_DIM + MLA_V_DIM
    for hh in range(acc.shape[1] // head_w):
        k_cols = slice(hh * head_w, hh * head_w + MLA_NOPE_DIM)
        v_cols = slice(hh * head_w + MLA_NOPE_DIM, (hh + 1) * head_w)
        kv_ref[:, k_cols] = _rms(acc[:, k_cols], gk_ref[...]).astype(kv_ref.dtype)
        kv_ref[:, v_cols] = acc[:, v_cols].astype(kv_ref.dtype)


def _mla_kv_proj(a, g_kv_a, w, layer, g_k_nope, g_pe_pad, tables, *, tm, tn):
    m = a.shape[0]
    k, n = w.shape[1:]
    cos, s1, s2 = tables
    tab = pl.BlockSpec((tm, LANES), lambda i, j: (i, 0))
    gain = pl.BlockSpec((None, 1, LANES), lambda i, j: (layer, 0, 0))
    kv_col_block = MLA_Q_RANK // MLA_KV_BLOCK
    return pl.pallas_call(
        _mla_kv_kernel,
        grid=(m // tm, n // tn),
        in_specs=[
            pl.BlockSpec((tm, MLA_KV_BLOCK), lambda i, j: (i, kv_col_block)),
            pl.BlockSpec((None, 1, k), lambda i, j: (layer, 0, 0)),
            pl.BlockSpec((None, k, tn), lambda i, j: (layer, 0, j)),
            gain, gain,
            tab, tab, tab,
        ],
        out_specs=[
            pl.BlockSpec((tm, tn), lambda i, j: (i, j)),
            pl.BlockSpec((tm, 2 * LANES), lambda i, j: (i, 0)),
        ],
        out_shape=[
            jax.ShapeDtypeStruct((m, n), BF16),
            jax.ShapeDtypeStruct((m, 2 * LANES), BF16),
        ],
        scratch_shapes=[pltpu.VMEM((tm, k), BF16)],
        compiler_params=_params("parallel", "arbitrary"),
        name="mla_kv_proj",
    )(a, g_kv_a, w, g_k_nope, g_pe_pad, cos, s1, s2)


def _softmax_pv(scores, values, scale):
    c = scale * LOG2_E
    m = None
    for s in scores:
        mi = jnp.max(s, axis=-1, keepdims=True)
        m = mi if m is None else jnp.maximum(m, mi)
    mc = m * c
    den = None
    acc = None
    for s, v in zip(scores, values):
        p = jnp.exp2(s * c - mc)
        li = jnp.sum(p, axis=-1, keepdims=True)
        ai = _dot(p.astype(v.dtype), v)
        den = li if den is None else den + li
        acc = ai if acc is None else acc + ai
    return acc / den


MLA_Q_BLOCK = 512


def _mla_attn_kernel(qn_ref, qpe_ref, kv_ref, kpe_ref, o_ref, kcat_ref):
    seq = qn_ref.shape[0]
    tq = MLA_Q_BLOCK
    scale = (MLA_NOPE_DIM + MLA_ROPE_DIM) ** -0.5
    head_w = MLA_NOPE_DIM + MLA_V_DIM
    for hh in range(2):
        kcat_ref[hh, :, :MLA_NOPE_DIM] = kv_ref[:, hh * head_w:hh * head_w + MLA_NOPE_DIM]
        kcat_ref[hh, :, MLA_NOPE_DIM:] = kpe_ref[:, hh * LANES:(hh + 1) * LANES]

    row = lax.broadcasted_iota(jnp.int32, (tq, tq), 0)
    col = lax.broadcasted_iota(jnp.int32, (tq, tq), 1)
    causal = col <= row
    def block_scores(qb, hh):
        lo = qb * tq
        q = jnp.concatenate([qn_ref[lo:lo + tq, hh * LANES:(hh + 1) * LANES], qpe_ref[lo:lo + tq, :]], axis=1)
        v0 = hh * head_w + MLA_NOPE_DIM
        scores = [jnp.where(causal, _dot_nt(q, kcat_ref[hh, lo:lo + tq, :]), -jnp.inf)]
        values = [kv_ref[lo:lo + tq, v0:v0 + MLA_V_DIM]]
        if qb > 0:
            scores.append(_dot_nt(q, kcat_ref[hh, 0:lo, :]))
            values.append(kv_ref[0:lo, v0:v0 + MLA_V_DIM])
        return scores, values

    order = [(qb, hh) for qb in range(seq // tq) for hh in range(2)]
    pending = block_scores(*order[0])
    for idx, (qb, hh) in enumerate(order):
        upcoming = block_scores(*order[idx + 1]) if idx + 1 < len(order) else None
        o = _softmax_pv(*pending, scale)
        o_ref[qb * tq:(qb + 1) * tq, hh * MLA_V_DIM:(hh + 1) * MLA_V_DIM] = o.astype(o_ref.dtype)
        pending = upcoming


def _mla_attention(q_nope, q_pe, kv, kpe, batch, seq):
    pairs = MLA_HEADS // 2
    return pl.pallas_call(
        _mla_attn_kernel,
        grid=(batch, pairs),
        in_specs=[
            pl.BlockSpec((seq, 2 * MLA_NOPE_DIM), lambda b, p: (b, p)),
            pl.BlockSpec((seq, LANES), lambda b, p: (b, p)),
            pl.BlockSpec((seq, 2 * (MLA_NOPE_DIM + MLA_V_DIM)), lambda b, p: (b, p)),
            pl.BlockSpec((seq, 2 * LANES), lambda b, p: (b, 0)),
        ],
        out_specs=pl.BlockSpec((seq, 2 * MLA_V_DIM), lambda b, p: (b, p)),
        out_shape=jax.ShapeDtypeStruct((batch * seq, MLA_HEADS * MLA_V_DIM), BF16),
        scratch_shapes=[pltpu.VMEM((2, seq, MLA_NOPE_DIM + LANES), BF16)],
        compiler_params=_params("parallel", "parallel"),
        name="mla_attention",
    )(q_nope, q_pe, kv, kpe)


def _dil_qk_kernel(n_j, x_ref, w_ref, gain_ref, cos_ref, sin_ref, o_ref, acc_ref):
    _deferred_prologue(n_j, acc_ref)
    for r0 in range(0, acc_ref.shape[0], EPILOGUE_ROWS):
        rows = slice(r0, r0 + EPILOGUE_ROWS)
        cos, sin = cos_ref[rows, :], sin_ref[rows, :]
        for hh in range(acc_ref.shape[1] // DIL_HEAD_DIM):
            cols = slice(hh * DIL_HEAD_DIM, (hh + 1) * DIL_HEAD_DIM)
            y = _rms(acc_ref[rows, cols], gain_ref[:, cols])
            o_ref[rows, cols] = _rope_dil(y, cos, sin).astype(o_ref.dtype)
    acc_ref[...] = _dot(x_ref[...], w_ref[...].astype(BF16))


def _dil_qk_proj(x, w, layer, gain_cols, tables, *, tm, tn):
    m, k = x.shape
    n = gain_cols.shape[-1]
    cos, sin = tables
    n_j = n // tn
    steps, cur, prev = _deferred_schedule(m // tm, n_j)
    tab = pl.BlockSpec((tm, LANES), lambda t: (prev(t)[0], 0))
    return pl.pallas_call(
        functools.partial(_dil_qk_kernel, n_j),
        grid=(steps + 1,),
        in_specs=[
            _row_block_spec(tm, k, lambda t: (cur(t)[0], 0), True),
            pl.BlockSpec((None, k, tn), lambda t: (layer, 0, cur(t)[1])),
            pl.BlockSpec((None, 1, tn), lambda t: (layer, 0, prev(t)[1])),
            tab, tab,
        ],
        out_specs=pl.BlockSpec((tm, tn), lambda t: prev(t)),
        out_shape=jax.ShapeDtypeStruct((m, n), BF16),
        scratch_shapes=[pltpu.VMEM((tm, tn), F32)],
        compiler_params=_params("arbitrary"),
        name="dil_qk_proj",
    )(x, w, gain_cols, cos, sin)


DIL_Q_BLOCK = 128


def _dil_bias(rows, cols, offset, window, dilation):
    dist = (offset + lax.broadcasted_iota(jnp.int32, (rows, cols), 0)
            - lax.broadcasted_iota(jnp.int32, (rows, cols), 1))
    valid = (dist >= 0) & (dist <= window) & ((dist & (dilation - 1)) == 0)
    return jnp.where(valid, 0.0, -jnp.inf).astype(F32)


def _dil_attn_kernel(*refs):
    n_groups = len(DIL_GROUPS)
    q_refs = refs[:n_groups]
    k_refs = refs[n_groups:2 * n_groups]
    v_refs = refs[2 * n_groups:3 * n_groups]
    o_ref = refs[3 * n_groups]
    seq = o_ref.shape[0]
    tq = DIL_Q_BLOCK
    scale = DIL_HEAD_DIM ** -0.5
    diag, full, cut = [], [], []
    for window, dilation in DIL_GROUPS:
        assert tq % dilation == 0 and window % tq == 0
        diag.append(_dil_bias(tq, tq, 0, window, dilation))
        full.append(_dil_bias(tq, window, window, window, dilation) if window < seq else None)
        n_cut = min(window, seq) - tq
        cut.append(_dil_bias(tq, n_cut, n_cut, seq, dilation) if n_cut > 0 and dilation > 1 else None)

    def block_scores(t0):
        scores, values = [], []
        for g, (window, dilation) in enumerate(DIL_GROUPS):
            q = q_refs[g][t0:t0 + tq, :]
            scores.append(_dot_nt(q, k_refs[g][t0:t0 + tq, :]) + diag[g])
            values.append(v_refs[g][t0:t0 + tq, :])
            n_prev = min(window, t0)
            if n_prev == 0:
                continue
            s_prev = _dot_nt(q, k_refs[g][t0 - n_prev:t0, :])
            if n_prev == window:
                s_prev = s_prev + full[g]
            elif cut[g] is not None:
                s_prev = s_prev + cut[g][:, cut[g].shape[1] - n_prev:]
            scores.append(s_prev)
            values.append(v_refs[g][t0 - n_prev:t0, :])
        return scores, values

    n_blocks = seq // tq
    pending = block_scores(0)
    for sb in range(n_blocks):
        upcoming = block_scores((sb + 1) * tq) if sb + 1 < n_blocks else None
        o = _softmax_pv(*pending, scale)
        o_ref[sb * tq:(sb + 1) * tq, :] = o.astype(o_ref.dtype)
        pending = upcoming


def _dil_attention(qk, v, batch, seq):
    n_groups = len(DIL_GROUPS)
    hd = DIL_HEAD_DIM

    def spec(part, g):
        c = (part * n_groups + g) * DIL_HEADS
        return pl.BlockSpec((seq, hd), lambda b, h: (b, c + h))

    return pl.pallas_call(
        _dil_attn_kernel,
        grid=(batch, DIL_HEADS),
        in_specs=[spec(part, g) for part in range(2) for g in range(n_groups)]
        + [spec(0, g) for g in range(n_groups)],
        out_specs=pl.BlockSpec((seq, hd), lambda b, h: (b, h)),
        out_shape=jax.ShapeDtypeStruct((batch * seq, DIL_HEADS * hd), BF16),
        compiler_params=_params("parallel", "parallel"),
        name="dil_attention",
    )(*([qk] * (2 * n_groups) + [v] * n_groups))


def _pad_last(x, width):
    return jnp.pad(x, [(0, 0)] * (x.ndim - 1) + [(0, width - x.shape[-1])])


def kernel(x, c, positions, w_cond, b_cond, w_mod, b_mod, g_mix_norm, g_ffn_norm, mla_w_in, mla_g_q_a, mla_g_kv_a, mla_w_q_b, mla_w_kv_b, mla_g_q_nope, mla_g_q_pe, mla_g_k_nope, mla_g_k_pe, mla_w_o, dil_w_qkv, dil_g_q, dil_g_k, dil_w_o, ffn_w_gate, ffn_w_up, ffn_w_down):
    batch, seq, d = x.shape
    depth = w_mod.shape[0]
    n_a = mla_w_in.shape[0]
    n_b = dil_w_qkv.shape[0]
    m = batch * seq
    hidden = ffn_w_gate.shape[-1]
    hidden_pad = -(-hidden // FFN_PAD_MULTIPLE) * FFN_PAD_MULTIPLE

    w_in = _pad_last(mla_w_in.astype(BF16), MLA_IN_PAD)
    qk_dim = MLA_NOPE_DIM + MLA_ROPE_DIM
    w_q_b = mla_w_q_b.astype(BF16).reshape(n_a, MLA_Q_RANK, MLA_HEADS, qk_dim)
    w_q_b = jnp.concatenate(
        [w_q_b[..., :MLA_NOPE_DIM].reshape(n_a, MLA_Q_RANK, MLA_HEADS * MLA_NOPE_DIM),
         w_q_b[..., MLA_NOPE_DIM:].reshape(n_a, MLA_Q_RANK, MLA_HEADS * MLA_ROPE_DIM)], axis=-1)
    w_kv_b = mla_w_kv_b.astype(BF16)
    w_o_a = mla_w_o.astype(BF16)
    w_o_b = dil_w_o.astype(BF16)

    g_mix = g_mix_norm[:, None, :]
    g_ffn = g_ffn_norm[:, None, :]
    g_q_a = mla_g_q_a[:, None, :]
    g_kv_a = mla_g_kv_a[:, None, :]
    g_q_nope = mla_g_q_nope[:, None, :]
    g_k_nope = mla_g_k_nope[:, None, :]
    g_q_pe = jnp.concatenate([mla_g_q_pe, mla_g_q_pe], axis=-1)[:, None, :]
    g_k_pe = _pad_last(mla_g_k_pe, LANES)[:, None, :]
    n_groups = len(DIL_GROUPS)
    n_qk = 2 * n_groups * DIL_HEADS * DIL_HEAD_DIM
    dil_gain = jnp.stack([dil_g_q, dil_g_k], axis=1)
    dil_gain = jnp.broadcast_to(dil_gain[:, :, :, None, :], (n_b, 2, n_groups, DIL_HEADS, DIL_HEAD_DIM))
    dil_gain = dil_gain.reshape(n_b, 1, n_qk)

    mod = _cond_mod(c, w_cond, b_cond, w_mod, b_mod)
    cos_m, s1_m, s2_m, cos_d, sin_d = _rope_tables(positions)

    xr = x.reshape(m, d)
    for i in range(depth):
        j = i // 2
        h = _norm_mod(xr, g_mix, mod, i, 0, 1, batch, seq)
        if i % 2 == 0:
            a = _matmul(h, w_in, j, tm=1024, tn=MLA_KV_BLOCK, out_dtype=F32)
            q_nope, q_pe = _mla_q_proj(a, g_q_a, w_q_b, j, g_q_nope, g_q_pe, (cos_m, s1_m, s2_m),
                                       tm=1024, tn=1024)
            kv, kpe = _mla_kv_proj(a, g_kv_a, w_kv_b, j, g_k_nope, g_k_pe, (cos_m, s1_m, s2_m),
                                   tm=1024, tn=1024)
            o = _mla_attention(q_nope, q_pe, kv, kpe, batch, seq)
            xr = _matmul_resid(o, w_o_a, j, xr, mod, i, 2, batch, seq, tm=1024, tn=512)
        else:
            qk = _dil_qk_proj(h, dil_w_qkv, j, dil_gain, (cos_d, sin_d), tm=2048, tn=512)
            v = _matmul(h, dil_w_qkv, j, tm=2048, tn=512, out_dtype=BF16, col0=n_qk,
                        single_buffer_x=True)
            o = _dil_attention(qk, v, batch, seq)
            xr = _matmul_resid(o, w_o_b, j, xr, mod, i, 2, batch, seq, tm=1024, tn=512)
        h = _norm_mod(xr, g_ffn, mod, i, 3, 4, batch, seq)
        act, w_down = _swiglu_up(h, ffn_w_gate, ffn_w_up, ffn_w_down, i, hidden_pad,
                                 tm=2048, tn=FFN_COL_BLOCK)
        xr = _matmul_resid(act, w_down[None], 0, xr, mod, i, 5, batch, seq, tm=1024, tn=1024,
                           tk=hidden_pad // 4)
    return xr.reshape(batch, seq, d)
```

```python
import functools
import math

import jax
import jax.numpy as jnp
from jax import lax
from jax.experimental import pallas as pl
from jax.experimental.pallas import tpu as pltpu

F32 = jnp.float32
BF16 = jnp.bfloat16

ROPE_THETA = 10000.0
NORM_EPS = 1e-6
N_MOD = 6

MLA_HEADS = 32
MLA_Q_RANK = 1536
MLA_KV_RANK = 512
MLA_NOPE_DIM = 128
MLA_ROPE_DIM = 64
MLA_V_DIM = 128

DIL_GROUPS = ((128, 1), (512, 4), (2048, 16))
DIL_HEADS = 16
DIL_HEAD_DIM = 128

LANES = 128
V7X_VMEM_LIMIT_BYTES = 56 * 2**20

FFN_PAD_MULTIPLE = 1024
EPILOGUE_ROWS = 256
FFN_COL_BLOCK = 256
MLA_IN_PAD = 2304
LOG2_E = math.log2(math.e)


def _params(*semantics):
    return pltpu.CompilerParams(dimension_semantics=semantics,
                                vmem_limit_bytes=V7X_VMEM_LIMIT_BYTES)


def _dot(a, b):
    return jnp.dot(a, b, preferred_element_type=F32)


def _dot_nt(a, b):
    return lax.dot_general(a, b, (((1,), (1,)), ((), ())), preferred_element_type=F32)


def _rms(x, gain, inv_n=None):
    if inv_n is None:
        ms = jnp.mean(x * x, axis=-1, keepdims=True)
    else:
        ms = jnp.sum(x * x, axis=-1, keepdims=True) * inv_n
    return x * lax.rsqrt(ms + NORM_EPS) * gain


def _cond_mod_kernel(c_ref, wc_ref, bc_ref, wm_ref, bm_ref, o_ref, e_ref):
    @pl.when((pl.program_id(0) == 0) & (pl.program_id(1) == 0))
    def _():
        z = _dot(c_ref[...].astype(BF16), wc_ref[...].astype(BF16)) + bc_ref[...]
        e_ref[...] = z * jax.nn.sigmoid(z)

    o_ref[...] = _dot(e_ref[...].astype(BF16), wm_ref[...].astype(BF16)) + bm_ref[...]


def _cond_mod(c, w_cond, b_cond, w_mod, b_mod):
    batch, d = c.shape
    depth, rank, n = w_mod.shape
    rows = 8
    tn = 2048
    c8 = jnp.pad(c, ((0, rows - batch), (0, 0)))
    out = pl.pallas_call(
        _cond_mod_kernel,
        grid=(depth, n // tn),
        in_specs=[
            pl.BlockSpec((rows, d), lambda l, j: (0, 0)),
            pl.BlockSpec((d, rank), lambda l, j: (0, 0)),
            pl.BlockSpec((1, rank), lambda l, j: (0, 0)),
            pl.BlockSpec((None, rank, tn), lambda l, j: (l, 0, j)),
            pl.BlockSpec((None, 1, tn), lambda l, j: (l, 0, j)),
        ],
        out_specs=pl.BlockSpec((None, rows, tn), lambda l, j: (l, 0, j)),
        out_shape=jax.ShapeDtypeStruct((depth, rows, n), F32),
        scratch_shapes=[pltpu.VMEM((rows, rank), F32)],
        compiler_params=_params("arbitrary", "arbitrary"),
        name="cond_mod",
    )(c8, w_cond, b_cond.reshape(1, rank), w_mod, b_mod.reshape(depth, 1, n))
    return out[:, :batch].reshape(depth * batch, 1, n)


def _rope_tables_kernel(pos_ref, cm_ref, s1_ref, s2_ref, cd_ref, sd_ref):
    pos = pos_ref[...].astype(F32)
    lane = lax.broadcasted_iota(jnp.int32, (1, LANES), 1)

    def angles(dim):
        half = dim // 2
        idx = (lane & (half - 1)).astype(F32)
        inv_freq = jnp.exp(idx * (-2.0 * math.log(ROPE_THETA) / dim))
        return pos * inv_freq

    ang = angles(MLA_ROPE_DIM)
    cos, sin = jnp.cos(ang), jnp.sin(ang)
    first_half = (lane & (MLA_ROPE_DIM - 1)) < MLA_ROPE_DIM // 2
    cm_ref[...] = cos
    s1_ref[...] = jnp.where(first_half, -sin, 0.0)
    s2_ref[...] = jnp.where(first_half, 0.0, sin)
    ang = angles(DIL_HEAD_DIM)
    cos, sin = jnp.cos(ang), jnp.sin(ang)
    cd_ref[...] = cos
    sd_ref[...] = jnp.where(lane < DIL_HEAD_DIM // 2, -sin, sin)


def _rope_tables(positions):
    batch, seq = positions.shape
    ts = 512
    m = batch * seq
    table = jax.ShapeDtypeStruct((m, LANES), F32)
    spec = pl.BlockSpec((ts, LANES), lambda i: (i, 0))
    return pl.pallas_call(
        _rope_tables_kernel,
        grid=(m // ts,),
        in_specs=[pl.BlockSpec((ts, 1), lambda i: (i, 0))],
        out_specs=[spec] * 5,
        out_shape=[table] * 5,
        compiler_params=_params("parallel"),
        name="rope_tables",
    )(positions.reshape(m, 1))


def _norm_mod_kernel(x_ref, g_ref, sc_ref, sh_ref, o_ref):
    y = _rms(x_ref[...], g_ref[...])
    o_ref[...] = (y * (1.0 + sc_ref[...]) + sh_ref[...]).astype(o_ref.dtype)


def _norm_mod(x, gain, mod, layer, shift_idx, scale_idx, batch, seq):
    m, d = x.shape
    tm = 512
    per_batch = seq // tm

    def mod_spec(which):
        return pl.BlockSpec((None, 1, d), lambda i: (layer * batch + i // per_batch, 0, which))

    return pl.pallas_call(
        _norm_mod_kernel,
        grid=(m // tm,),
        in_specs=[
            pl.BlockSpec((tm, d), lambda i: (i, 0)),
            pl.BlockSpec((None, 1, d), lambda i: (layer, 0, 0)),
            mod_spec(scale_idx),
            mod_spec(shift_idx),
        ],
        out_specs=pl.BlockSpec((tm, d), lambda i: (i, 0)),
        out_shape=jax.ShapeDtypeStruct((m, d), BF16),
        compiler_params=_params("parallel"),
        name="norm_mod",
    )(x, gain, mod, mod)


def _mm_kernel(x_ref, w_ref, o_ref):
    o_ref[...] = _dot(x_ref[...], w_ref[...].astype(BF16)).astype(o_ref.dtype)


def _row_block_spec(tm, k, index_map, single_buffer):
    if single_buffer:
        return pl.BlockSpec((tm, k), index_map, pipeline_mode=pl.Buffered(1))
    return pl.BlockSpec((tm, k), index_map)


def _matmul(x, w, layer, *, tm, tn, out_dtype, col0=0, single_buffer_x=False):
    m, k = x.shape
    n = w.shape[-1] - col0
    col_block0 = col0 // tn
    return pl.pallas_call(
        _mm_kernel,
        grid=(m // tm, n // tn),
        in_specs=[
            _row_block_spec(tm, k, lambda i, j: (i, 0), single_buffer_x),
            pl.BlockSpec((None, k, tn), lambda i, j: (layer, 0, col_block0 + j)),
        ],
        out_specs=pl.BlockSpec((tm, tn), lambda i, j: (i, j)),
        out_shape=jax.ShapeDtypeStruct((m, n), out_dtype),
        compiler_params=_params("parallel", "arbitrary"),
        name="matmul",
    )(x, w)


def _mm_resid_kernel(x_ref, w_ref, r_ref, gt_ref, o_ref):
    o_ref[...] = r_ref[...] + gt_ref[...] * _dot(x_ref[...], w_ref[...])


def _mm_resid_ksplit_kernel(x_ref, w_ref, r_ref, gt_ref, o_ref):
    @pl.when(pl.program_id(2) == 0)
    def _():
        o_ref[...] = r_ref[...] + gt_ref[...] * _dot(x_ref[...], w_ref[...])

    @pl.when(pl.program_id(2) > 0)
    def _():
        o_ref[...] += gt_ref[...] * _dot(x_ref[...], w_ref[...])


def _matmul_resid(x, w, layer, resid, mod, mod_layer, gate_idx, batch, seq, *, tm, tn, tk=None):
    m, k = x.shape
    n = w.shape[-1]
    per_batch = seq // tm
    gate_blocks = n // tn
    if tk is None:
        return pl.pallas_call(
            _mm_resid_kernel,
            grid=(m // tm, n // tn),
            in_specs=[
                pl.BlockSpec((tm, k), lambda i, j: (i, 0)),
                pl.BlockSpec((None, k, tn), lambda i, j: (layer, 0, j)),
                pl.BlockSpec((tm, tn), lambda i, j: (i, j)),
                pl.BlockSpec((None, 1, tn), lambda i, j: (mod_layer * batch + i // per_batch, 0,
                                                          gate_idx * gate_blocks + j)),
            ],
            out_specs=pl.BlockSpec((tm, tn), lambda i, j: (i, j)),
            out_shape=jax.ShapeDtypeStruct((m, n), F32),
            compiler_params=_params("parallel", "arbitrary"),
            name="matmul_resid",
        )(x, w, resid, mod)
    return pl.pallas_call(
        _mm_resid_ksplit_kernel,
        grid=(m // tm, n // tn, k // tk),
        in_specs=[
            pl.BlockSpec((tm, tk), lambda i, j, kk: (i, kk)),
            pl.BlockSpec((None, tk, tn), lambda i, j, kk: (layer, kk, j)),
            pl.BlockSpec((tm, tn), lambda i, j, kk: (i, j)),
            pl.BlockSpec((None, 1, tn), lambda i, j, kk: (mod_layer * batch + i // per_batch, 0,
                                                          gate_idx * gate_blocks + j)),
        ],
        out_specs=pl.BlockSpec((tm, tn), lambda i, j, kk: (i, j)),
        out_shape=jax.ShapeDtypeStruct((m, n), F32),
        compiler_params=_params("parallel", "parallel", "arbitrary"),
        name="matmul_resid_ksplit",
    )(x, w, resid, mod)


def _swiglu_kernel(n_src_blocks, x_ref, wg_ref, wu_ref, wd_ref, o_ref, wd_out_ref):
    first_sweep = pl.program_id(0) == 0
    in_range = pl.program_id(1) < n_src_blocks

    @pl.when(in_range)
    def _():
        x = x_ref[...]
        g = _dot(x, wg_ref[...].astype(BF16))
        u = _dot(x, wu_ref[...].astype(BF16))
        o_ref[...] = (g * jax.nn.sigmoid(g) * u).astype(o_ref.dtype)

    @pl.when(jnp.logical_not(in_range))
    def _():
        o_ref[...] = jnp.zeros_like(o_ref)

    @pl.when(first_sweep & in_range)
    def _():
        wd_out_ref[...] = wd_ref[...].astype(wd_out_ref.dtype)

    @pl.when(first_sweep & jnp.logical_not(in_range))
    def _():
        wd_out_ref[...] = jnp.zeros_like(wd_out_ref)


def _swiglu_up(x, wg, wu, wd, layer, n_out, *, tm, tn):
    m, k = x.shape
    d = wd.shape[-1]
    n_src_blocks = wg.shape[-1] // tn
    n_dst_blocks = n_out // tn
    w_spec = pl.BlockSpec((None, k, tn), lambda i, j: (layer, 0, jnp.minimum(j, n_src_blocks - 1)))
    wd_in = pl.BlockSpec((None, tn, d), lambda i, j: (
        layer, jnp.where(i == 0, jnp.minimum(j, n_src_blocks - 1), n_src_blocks - 1), 0))
    wd_out = pl.BlockSpec((tn, d), lambda i, j: (jnp.where(i == 0, j, n_dst_blocks - 1), 0))
    return pl.pallas_call(
        functools.partial(_swiglu_kernel, n_src_blocks),
        grid=(m // tm, n_dst_blocks),
        in_specs=[_row_block_spec(tm, k, lambda i, j: (i, 0), True), w_spec, w_spec, wd_in],
        out_specs=[pl.BlockSpec((tm, tn), lambda i, j: (i, j)), wd_out],
        out_shape=[jax.ShapeDtypeStruct((m, n_out), BF16), jax.ShapeDtypeStruct((n_out, d), BF16)],
        compiler_params=_params("arbitrary", "arbitrary"),
        name="swiglu_up",
    )(x, wg, wu, wd)


def _rope_mla(x, cos, s1, s2):
    half = MLA_ROPE_DIM // 2
    return (x * cos + pltpu.roll(x, LANES - half, 1) * s1 + pltpu.roll(x, half, 1) * s2)


def _rope_dil(x, cos, sin_signed):
    return x * cos + pltpu.roll(x, DIL_HEAD_DIM // 2, 1) * sin_signed


def _deferred_schedule(n_i, n_j):
    steps = n_i * n_j

    def cur(t):
        t = jnp.minimum(t, steps - 1)
        return t // n_j, t % n_j

    def prev(t):
        t = jnp.maximum(t - 1, 0)
        return t // n_j, t % n_j

    return steps, cur, prev


def _deferred_prologue(n_j, acc_ref):
    t = pl.program_id(0)

    @pl.when(t == 0)
    def _():
        acc_ref[...] = jnp.zeros_like(acc_ref)

    return lax.rem(t, n_j) == 0


def _mla_q_nope_kernel(n_j, cq_ref, gqa_ref, w_ref, gn_ref, o_ref, hs_ref, acc_ref):
    @pl.when(_deferred_prologue(n_j, acc_ref))
    def _():
        hs_ref[...] = _rms(cq_ref[...], gqa_ref[...]).astype(hs_ref.dtype)

    for t in range(acc_ref.shape[1] // LANES):
        cols = slice(t * LANES, (t + 1) * LANES)
        o_ref[:, cols] = _rms(acc_ref[:, cols], gn_ref[...]).astype(o_ref.dtype)
    acc_ref[...] = _dot(hs_ref[...], w_ref[...])


def _mla_q_pe_kernel(n_j, cq_ref, gqa_ref, w_ref, gp_ref, cos_ref, s1_ref, s2_ref, o_ref, hs_ref, acc_ref):
    @pl.when(_deferred_prologue(n_j, acc_ref))
    def _():
        hs_ref[...] = _rms(cq_ref[...], gqa_ref[...]).astype(hs_ref.dtype)

    cos, s1, s2 = cos_ref[...], s1_ref[...], s2_ref[...]
    low = lax.broadcasted_iota(jnp.int32, (1, LANES), 1) < MLA_ROPE_DIM
    for t in range(acc_ref.shape[1] // LANES):
        cols = slice(t * LANES, (t + 1) * LANES)
        x = acc_ref[:, cols]
        sq = x * x
        ss_low = jnp.sum(jnp.where(low, sq, 0.0), axis=-1, keepdims=True)
        ss_high = jnp.sum(jnp.where(low, 0.0, sq), axis=-1, keepdims=True)
        ms = jnp.where(low, ss_low, ss_high) * (1.0 / MLA_ROPE_DIM)
        y = x * lax.rsqrt(ms + NORM_EPS) * gp_ref[...]
        o_ref[:, cols] = _rope_mla(y, cos, s1, s2).astype(o_ref.dtype)
    acc_ref[...] = _dot(hs_ref[...], w_ref[...])


def _mla_q_proj(a, g_q_a, w, layer, g_nope, g_pe_pair, tables, *, tm, tn):
    m = a.shape[0]
    k = w.shape[1]
    n_nope = MLA_HEADS * MLA_NOPE_DIM
    n_pe = MLA_HEADS * MLA_ROPE_DIM
    cos, s1, s2 = tables
    gain = pl.BlockSpec((None, 1, LANES), lambda t: (layer, 0, 0))

    def call(body, name, n_out, col_block0, with_tables, extra_args):
        n_j = n_out // tn
        steps, cur, prev = _deferred_schedule(m // tm, n_j)
        tab = pl.BlockSpec((tm, LANES), lambda t: (prev(t)[0], 0))
        return pl.pallas_call(
            functools.partial(body, n_j),
            grid=(steps + 1,),
            in_specs=[
                pl.BlockSpec((tm, k), lambda t: (cur(t)[0], 0)),
                pl.BlockSpec((None, 1, k), lambda t: (layer, 0, 0)),
                pl.BlockSpec((None, k, tn), lambda t: (layer, 0, col_block0 + cur(t)[1])),
                gain,
            ] + ([tab, tab, tab] if with_tables else []),
            out_specs=pl.BlockSpec((tm, tn), lambda t: prev(t)),
            out_shape=jax.ShapeDtypeStruct((m, n_out), BF16),
            scratch_shapes=[pltpu.VMEM((tm, k), BF16), pltpu.VMEM((tm, tn), F32)],
            compiler_params=_params("arbitrary"),
            name=name,
        )(a, g_q_a, w, *extra_args)

    q_nope = call(_mla_q_nope_kernel, "mla_q_nope_proj", n_nope, 0, False, [g_nope])
    q_pe = call(_mla_q_pe_kernel, "mla_q_pe_proj", n_pe, n_nope // tn, True, [g_pe_pair, cos, s1, s2])
    return q_nope, q_pe


MLA_KV_BLOCK = 768


def _mla_kv_kernel(ckv_ref, gkva_ref, w_ref, gk_ref, gp_ref, cos_ref, s1_ref, s2_ref,
                   kv_ref, kpe_ref, hs_ref):
    @pl.when(pl.program_id(1) == 0)
    def _():
        blk = ckv_ref[...]
        hs_ref[...] = _rms(blk[:, :MLA_KV_RANK], gkva_ref[...]).astype(hs_ref.dtype)
        pe = _rms(blk[:, MLA_KV_RANK:MLA_KV_RANK + LANES], gp_ref[...], inv_n=1.0 / MLA_ROPE_DIM)
        pe = _rope_mla(pe, cos_ref[...], s1_ref[...], s2_ref[...])
        kpe_ref[:, :LANES] = pe.astype(kpe_ref.dtype)
        kpe_ref[:, LANES:] = pltpu.roll(pe, MLA_ROPE_DIM, 1).astype(kpe_ref.dtype)

    acc = _dot(hs_ref[...], w_ref[...])
    head_w = MLA_NOPE_DIM + MLA_V_DIM
    for hh in range(acc.shape[1] // head_w):
        k_cols = slice(hh * head_w, hh * head_w + MLA_NOPE_DIM)
        v_cols = slice(hh * head_w + MLA_NOPE_DIM, (hh + 1) * head_w)
        kv_ref[:, k_cols] = _rms(acc[:, k_cols], gk_ref[...]).astype(kv_ref.dtype)
        kv_ref[:, v_cols] = acc[:, v_cols].astype(kv_ref.dtype)


def _mla_kv_proj(a, g_kv_a, w, layer, g_k_nope, g_pe_pad, tables, *, tm, tn):
    m = a.shape[0]
    k, n = w.shape[1:]
    cos, s1, s2 = tables
    tab = pl.BlockSpec((tm, LANES), lambda i, j: (i, 0))
    gain = pl.BlockSpec((None, 1, LANES), lambda i, j: (layer, 0, 0))
    kv_col_block = MLA_Q_RANK // MLA_KV_BLOCK
    return pl.pallas_call(
        _mla_kv_kernel,
        grid=(m // tm, n // tn),
        in_specs=[
            pl.BlockSpec((tm, MLA_KV_BLOCK), lambda i, j: (i, kv_col_block)),
            pl.BlockSpec((None, 1, k), lambda i, j: (layer, 0, 0)),
            pl.BlockSpec((None, k, tn), lambda i, j: (layer, 0, j)),
            gain, gain,
            tab, tab, tab,
        ],
        out_specs=[
            pl.BlockSpec((tm, tn), lambda i, j: (i, j)),
            pl.BlockSpec((tm, 2 * LANES), lambda i, j: (i, 0)),
        ],
        out_shape=[
            jax.ShapeDtypeStruct((m, n), BF16),
            jax.ShapeDtypeStruct((m, 2 * LANES), BF16),
        ],
        scratch_shapes=[pltpu.VMEM((tm, k), BF16)],
        compiler_params=_params("parallel", "arbitrary"),
        name="mla_kv_proj",
    )(a, g_kv_a, w, g_k_nope, g_pe_pad, cos, s1, s2)


def _softmax_pv(scores, values, scale):
    c = scale * LOG2_E
    m = None
    for s in scores:
        mi = jnp.max(s, axis=-1, keepdims=True)
        m = mi if m is None else jnp.maximum(m, mi)
    mc = m * c
    den = None
    acc = None
    for s, v in zip(scores, values):
        p = jnp.exp2(s * c - mc)
        li = jnp.sum(p, axis=-1, keepdims=True)
        ai = _dot(p.astype(v.dtype), v)
        den = li if den is None else den + li
        acc = ai if acc is None else acc + ai
    return acc / den


MLA_Q_BLOCK = 512


def _mla_attn_kernel(qn_ref, qpe_ref, kv_ref, kpe_ref, wo_ref, o_ref, wo_out_ref, kcat_ref):
    wo_out_ref[...] = wo_ref[...].astype(wo_out_ref.dtype)
    seq = qn_ref.shape[0]
    tq = MLA_Q_BLOCK
    scale = (MLA_NOPE_DIM + MLA_ROPE_DIM) ** -0.5
    head_w = MLA_NOPE_DIM + MLA_V_DIM
    for hh in range(2):
        kcat_ref[hh, :, :MLA_NOPE_DIM] = kv_ref[:, hh * head_w:hh * head_w + MLA_NOPE_DIM]
        kcat_ref[hh, :, MLA_NOPE_DIM:] = kpe_ref[:, hh * LANES:(hh + 1) * LANES]

    row = lax.broadcasted_iota(jnp.int32, (tq, tq), 0)
    col = lax.broadcasted_iota(jnp.int32, (tq, tq), 1)
    causal = col <= row
    def block_scores(qb, hh):
        lo = qb * tq
        q = jnp.concatenate([qn_ref[lo:lo + tq, hh * LANES:(hh + 1) * LANES], qpe_ref[lo:lo + tq, :]], axis=1)
        v0 = hh * head_w + MLA_NOPE_DIM
        scores = [jnp.where(causal, _dot_nt(q, kcat_ref[hh, lo:lo + tq, :]), -jnp.inf)]
        values = [kv_ref[lo:lo + tq, v0:v0 + MLA_V_DIM]]
        if qb > 0:
            scores.append(_dot_nt(q, kcat_ref[hh, 0:lo, :]))
            values.append(kv_ref[0:lo, v0:v0 + MLA_V_DIM])
        return scores, values

    order = [(qb, hh) for qb in range(seq // tq) for hh in range(2)]
    pending = block_scores(*order[0])
    for idx, (qb, hh) in enumerate(order):
        upcoming = block_scores(*order[idx + 1]) if idx + 1 < len(order) else None
        o = _softmax_pv(*pending, scale)
        o_ref[qb * tq:(qb + 1) * tq, hh * MLA_V_DIM:(hh + 1) * MLA_V_DIM] = o.astype(o_ref.dtype)
        pending = upcoming


def _mla_attention(q_nope, q_pe, kv, kpe, w_o, layer, batch, seq):
    pairs = MLA_HEADS // 2
    wo_rows, wo_cols = w_o.shape[1:]
    slab = wo_rows // (batch * pairs)
    assert slab * batch * pairs == wo_rows
    return pl.pallas_call(
        _mla_attn_kernel,
        grid=(batch, pairs),
        in_specs=[
            pl.BlockSpec((seq, 2 * MLA_NOPE_DIM), lambda b, p: (b, p)),
            pl.BlockSpec((seq, LANES), lambda b, p: (b, p)),
            pl.BlockSpec((seq, 2 * (MLA_NOPE_DIM + MLA_V_DIM)), lambda b, p: (b, p)),
            pl.BlockSpec((seq, 2 * LANES), lambda b, p: (b, 0)),
            pl.BlockSpec((None, slab, wo_cols), lambda b, p: (layer, b * pairs + p, 0)),
        ],
        out_specs=[
            pl.BlockSpec((seq, 2 * MLA_V_DIM), lambda b, p: (b, p)),
            pl.BlockSpec((slab, wo_cols), lambda b, p: (b * pairs + p, 0)),
        ],
        out_shape=[
            jax.ShapeDtypeStruct((batch * seq, MLA_HEADS * MLA_V_DIM), BF16),
            jax.ShapeDtypeStruct((wo_rows, wo_cols), BF16),
        ],
        scratch_shapes=[pltpu.VMEM((2, seq, MLA_NOPE_DIM + LANES), BF16)],
        compiler_params=_params("parallel", "parallel"),
        name="mla_attention",
    )(q_nope, q_pe, kv, kpe, w_o)


def _dil_qk_kernel(n_j, x_ref, w_ref, gain_ref, cos_ref, sin_ref, o_ref, acc_ref):
    _deferred_prologue(n_j, acc_ref)
    for r0 in range(0, acc_ref.shape[0], EPILOGUE_ROWS):
        rows = slice(r0, r0 + EPILOGUE_ROWS)
        cos, sin = cos_ref[rows, :], sin_ref[rows, :]
        for hh in range(acc_ref.shape[1] // DIL_HEAD_DIM):
            cols = slice(hh * DIL_HEAD_DIM, (hh + 1) * DIL_HEAD_DIM)
            y = _rms(acc_ref[rows, cols], gain_ref[:, cols])
            o_ref[rows, cols] = _rope_dil(y, cos, sin).astype(o_ref.dtype)
    acc_ref[...] = _dot(x_ref[...], w_ref[...].astype(BF16))


def _dil_qk_proj(x, w, layer, gain_cols, tables, *, tm, tn):
    m, k = x.shape
    n = gain_cols.shape[-1]
    cos, sin = tables
    n_j = n // tn
    steps, cur, prev = _deferred_schedule(m // tm, n_j)
    tab = pl.BlockSpec((tm, LANES), lambda t: (prev(t)[0], 0))
    return pl.pallas_call(
        functools.partial(_dil_qk_kernel, n_j),
        grid=(steps + 1,),
        in_specs=[
            _row_block_spec(tm, k, lambda t: (cur(t)[0], 0), True),
            pl.BlockSpec((None, k, tn), lambda t: (layer, 0, cur(t)[1])),
            pl.BlockSpec((None, 1, tn), lambda t: (layer, 0, prev(t)[1])),
            tab, tab,
        ],
        out_specs=pl.BlockSpec((tm, tn), lambda t: prev(t)),
        out_shape=jax.ShapeDtypeStruct((m, n), BF16),
        scratch_shapes=[pltpu.VMEM((tm, tn), F32)],
        compiler_params=_params("arbitrary"),
        name="dil_qk_proj",
    )(x, w, gain_cols, cos, sin)


DIL_Q_BLOCK = 128


def _dil_bias(rows, cols, offset, window, dilation):
    dist = (offset + lax.broadcasted_iota(jnp.int32, (rows, cols), 0)
            - lax.broadcasted_iota(jnp.int32, (rows, cols), 1))
    valid = (dist >= 0) & (dist <= window) & ((dist & (dilation - 1)) == 0)
    return jnp.where(valid, 0.0, -jnp.inf).astype(F32)


def _dil_attn_kernel(*refs):
    n_groups = len(DIL_GROUPS)
    q_refs = refs[:n_groups]
    k_refs = refs[n_groups:2 * n_groups]
    v_refs = refs[2 * n_groups:3 * n_groups]
    wo_ref, o_ref, wo_out_ref = refs[3 * n_groups:]
    wo_out_ref[...] = wo_ref[...].astype(wo_out_ref.dtype)
    seq = o_ref.shape[0]
    tq = DIL_Q_BLOCK
    scale = DIL_HEAD_DIM ** -0.5
    diag, full, cut = [], [], []
    for window, dilation in DIL_GROUPS:
        assert tq % dilation == 0 and (window % tq == 0 or tq % window == 0)
        diag.append(_dil_bias(tq, tq, 0, window, dilation))
        full.append(_dil_bias(tq, window, window, window, dilation) if window < seq else None)
        n_cut = min(window, seq) - tq
        cut.append(_dil_bias(tq, n_cut, n_cut, seq, dilation) if n_cut > 0 and dilation > 1 else None)

    def block_scores(t0):
        scores, values = [], []
        for g, (window, dilation) in enumerate(DIL_GROUPS):
            q = q_refs[g][t0:t0 + tq, :]
            scores.append(_dot_nt(q, k_refs[g][t0:t0 + tq, :]) + diag[g])
            values.append(v_refs[g][t0:t0 + tq, :])
            n_prev = min(window, t0)
            if n_prev == 0:
                continue
            s_prev = _dot_nt(q, k_refs[g][t0 - n_prev:t0, :])
            if n_prev == window:
                s_prev = s_prev + full[g]
            elif cut[g] is not None:
                s_prev = s_prev + cut[g][:, cut[g].shape[1] - n_prev:]
            scores.append(s_prev)
            values.append(v_refs[g][t0 - n_prev:t0, :])
        return scores, values

    n_blocks = seq // tq
    pending = block_scores(0)
    for sb in range(n_blocks):
        upcoming = block_scores((sb + 1) * tq) if sb + 1 < n_blocks else None
        o = _softmax_pv(*pending, scale)
        o_ref[sb * tq:(sb + 1) * tq, :] = o.astype(o_ref.dtype)
        pending = upcoming


def _dil_attention(qk, v, w_o, layer, batch, seq):
    n_groups = len(DIL_GROUPS)
    hd = DIL_HEAD_DIM
    wo_rows, wo_cols = w_o.shape[1:]
    slab = wo_rows // (batch * DIL_HEADS)
    assert slab * batch * DIL_HEADS == wo_rows

    def spec(part, g):
        c = (part * n_groups + g) * DIL_HEADS
        return pl.BlockSpec((seq, hd), lambda b, h: (b, c + h))

    return pl.pallas_call(
        _dil_attn_kernel,
        grid=(batch, DIL_HEADS),
        in_specs=[spec(part, g) for part in range(2) for g in range(n_groups)]
        + [spec(0, g) for g in range(n_groups)]
        + [pl.BlockSpec((None, slab, wo_cols), lambda b, h: (layer, b * DIL_HEADS + h, 0))],
        out_specs=[
            pl.BlockSpec((seq, hd), lambda b, h: (b, h)),
            pl.BlockSpec((slab, wo_cols), lambda b, h: (b * DIL_HEADS + h, 0)),
        ],
        out_shape=[
            jax.ShapeDtypeStruct((batch * seq, DIL_HEADS * hd), BF16),
            jax.ShapeDtypeStruct((wo_rows, wo_cols), BF16),
        ],
        compiler_params=_params("parallel", "parallel"),
        name="dil_attention",
    )(*([qk] * (2 * n_groups) + [v] * n_groups + [w_o]))


def _pad_last(x, width):
    return jnp.pad(x, [(0, 0)] * (x.ndim - 1) + [(0, width - x.shape[-1])])


def kernel(x, c, positions, w_cond, b_cond, w_mod, b_mod, g_mix_norm, g_ffn_norm, mla_w_in, mla_g_q_a, mla_g_kv_a, mla_w_q_b, mla_w_kv_b, mla_g_q_nope, mla_g_q_pe, mla_g_k_nope, mla_g_k_pe, mla_w_o, dil_w_qkv, dil_g_q, dil_g_k, dil_w_o, ffn_w_gate, ffn_w_up, ffn_w_down):
    batch, seq, d = x.shape
    depth = w_mod.shape[0]
    n_a = mla_w_in.shape[0]
    n_b = dil_w_qkv.shape[0]
    m = batch * seq
    hidden = ffn_w_gate.shape[-1]
    hidden_pad = -(-hidden // FFN_PAD_MULTIPLE) * FFN_PAD_MULTIPLE

    w_in = _pad_last(mla_w_in.astype(BF16), MLA_IN_PAD)
    qk_dim = MLA_NOPE_DIM + MLA_ROPE_DIM
    w_q_b = mla_w_q_b.astype(BF16).reshape(n_a, MLA_Q_RANK, MLA_HEADS, qk_dim)
    w_q_b = jnp.concatenate(
        [w_q_b[..., :MLA_NOPE_DIM].reshape(n_a, MLA_Q_RANK, MLA_HEADS * MLA_NOPE_DIM),
         w_q_b[..., MLA_NOPE_DIM:].reshape(n_a, MLA_Q_RANK, MLA_HEADS * MLA_ROPE_DIM)], axis=-1)
    w_kv_b = mla_w_kv_b.astype(BF16)

    g_mix = g_mix_norm[:, None, :]
    g_ffn = g_ffn_norm[:, None, :]
    g_q_a = mla_g_q_a[:, None, :]
    g_kv_a = mla_g_kv_a[:, None, :]
    g_q_nope = mla_g_q_nope[:, None, :]
    g_k_nope = mla_g_k_nope[:, None, :]
    g_q_pe = jnp.concatenate([mla_g_q_pe, mla_g_q_pe], axis=-1)[:, None, :]
    g_k_pe = _pad_last(mla_g_k_pe, LANES)[:, None, :]
    n_groups = len(DIL_GROUPS)
    n_qk = 2 * n_groups * DIL_HEADS * DIL_HEAD_DIM
    dil_gain = jnp.stack([dil_g_q, dil_g_k], axis=1)
    dil_gain = jnp.broadcast_to(dil_gain[:, :, :, None, :], (n_b, 2, n_groups, DIL_HEADS, DIL_HEAD_DIM))
    dil_gain = dil_gain.reshape(n_b, 1, n_qk)

    mod = _cond_mod(c, w_cond, b_cond, w_mod, b_mod)
    cos_m, s1_m, s2_m, cos_d, sin_d = _rope_tables(positions)

    xr = x.reshape(m, d)
    for i in range(depth):
        j = i // 2
        h = _norm_mod(xr, g_mix, mod, i, 0, 1, batch, seq)
        if i % 2 == 0:
            a = _matmul(h, w_in, j, tm=1024, tn=MLA_KV_BLOCK, out_dtype=F32)
            q_nope, q_pe = _mla_q_proj(a, g_q_a, w_q_b, j, g_q_nope, g_q_pe, (cos_m, s1_m, s2_m),
                                       tm=1024, tn=1024)
            kv, kpe = _mla_kv_proj(a, g_kv_a, w_kv_b, j, g_k_nope, g_k_pe, (cos_m, s1_m, s2_m),
                                   tm=1024, tn=1024)
            o, w_o = _mla_attention(q_nope, q_pe, kv, kpe, mla_w_o, j, batch, seq)
            xr = _matmul_resid(o, w_o[None], 0, xr, mod, i, 2, batch, seq, tm=1024, tn=512)
        else:
            qk = _dil_qk_proj(h, dil_w_qkv, j, dil_gain, (cos_d, sin_d), tm=2048, tn=512)
            v = _matmul(h, dil_w_qkv, j, tm=2048, tn=512, out_dtype=BF16, col0=n_qk,
                        single_buffer_x=True)
            o, w_o = _dil_attention(qk, v, dil_w_o, j, batch, seq)
            xr = _matmul_resid(o, w_o[None], 0, xr, mod, i, 2, batch, seq, tm=1024, tn=1024)
        h = _norm_mod(xr, g_ffn, mod, i, 3, 4, batch, seq)
        act, w_down = _swiglu_up(h, ffn_w_gate, ffn_w_up, ffn_w_down, i, hidden_pad,
                                 tm=2048, tn=FFN_COL_BLOCK)
        xr = _matmul_resid(act, w_down[None], 0, xr, mod, i, 5, batch, seq, tm=1024, tn=1024,
                           tk=hidden_pad // 4)
    return xr.reshape(batch, seq, d)
```

```python
import functools
import math

import jax
import jax.numpy as jnp
from jax import lax
from jax.experimental import pallas as pl
from jax.experimental.pallas import tpu as pltpu

F32 = jnp.float32
BF16 = jnp.bfloat16

ROPE_THETA = 10000.0
NORM_EPS = 1e-6
N_MOD = 6

MLA_HEADS = 32
MLA_Q_RANK = 1536
MLA_KV_RANK = 512
MLA_NOPE_DIM = 128
MLA_ROPE_DIM = 64
MLA_V_DIM = 128

DIL_GROUPS = ((128, 1), (512, 4), (2048, 16))
DIL_HEADS = 16
DIL_HEAD_DIM = 128

LANES = 128
V7X_VMEM_LIMIT_BYTES = 56 * 2**20

FFN_PAD_MULTIPLE = 1024
EPILOGUE_ROWS = 256
FFN_COL_BLOCK = 256
MLA_IN_PAD = 2304
LOG2_E = math.log2(math.e)


def _params(*semantics):
    return pltpu.CompilerParams(dimension_semantics=semantics,
                                vmem_limit_bytes=V7X_VMEM_LIMIT_BYTES)


def _dot(a, b):
    return jnp.dot(a, b, preferred_element_type=F32)


def _dot_nt(a, b):
    return lax.dot_general(a, b, (((1,), (1,)), ((), ())), preferred_element_type=F32)


def _rms(x, gain, inv_n=None):
    if inv_n is None:
        ms = jnp.mean(x * x, axis=-1, keepdims=True)
    else:
        ms = jnp.sum(x * x, axis=-1, keepdims=True) * inv_n
    return x * lax.rsqrt(ms + NORM_EPS) * gain


def _cond_mod_kernel(c_ref, wc_ref, bc_ref, wm_ref, bm_ref, o_ref, e_ref):
    @pl.when((pl.program_id(0) == 0) & (pl.program_id(1) == 0))
    def _():
        z = _dot(c_ref[...].astype(BF16), wc_ref[...].astype(BF16)) + bc_ref[...]
        e_ref[...] = z * jax.nn.sigmoid(z)

    o_ref[...] = _dot(e_ref[...].astype(BF16), wm_ref[...].astype(BF16)) + bm_ref[...]


def _cond_mod(c, w_cond, b_cond, w_mod, b_mod):
    batch, d = c.shape
    depth, rank, n = w_mod.shape
    rows = 8
    tn = 2048
    c8 = jnp.pad(c, ((0, rows - batch), (0, 0)))
    out = pl.pallas_call(
        _cond_mod_kernel,
        grid=(depth, n // tn),
        in_specs=[
            pl.BlockSpec((rows, d), lambda l, j: (0, 0)),
            pl.BlockSpec((d, rank), lambda l, j: (0, 0)),
            pl.BlockSpec((1, rank), lambda l, j: (0, 0)),
            pl.BlockSpec((None, rank, tn), lambda l, j: (l, 0, j)),
            pl.BlockSpec((None, 1, tn), lambda l, j: (l, 0, j)),
        ],
        out_specs=pl.BlockSpec((None, rows, tn), lambda l, j: (l, 0, j)),
        out_shape=jax.ShapeDtypeStruct((depth, rows, n), F32),
        scratch_shapes=[pltpu.VMEM((rows, rank), F32)],
        compiler_params=_params("arbitrary", "arbitrary"),
        name="cond_mod",
    )(c8, w_cond, b_cond.reshape(1, rank), w_mod, b_mod.reshape(depth, 1, n))
    return out[:, :batch].reshape(depth * batch, 1, n)


def _rope_tables_kernel(pos_ref, cm_ref, s1_ref, s2_ref, cd_ref, sd_ref):
    pos = pos_ref[...].astype(F32)
    lane = lax.broadcasted_iota(jnp.int32, (1, LANES), 1)

    def angles(dim):
        half = dim // 2
        idx = (lane & (half - 1)).astype(F32)
        inv_freq = jnp.exp(idx * (-2.0 * math.log(ROPE_THETA) / dim))
        return pos * inv_freq

    ang = angles(MLA_ROPE_DIM)
    cos, sin = jnp.cos(ang), jnp.sin(ang)
    first_half = (lane & (MLA_ROPE_DIM - 1)) < MLA_ROPE_DIM // 2
    cm_ref[...] = cos
    s1_ref[...] = jnp.where(first_half, -sin, 0.0)
    s2_ref[...] = jnp.where(first_half, 0.0, sin)
    ang = angles(DIL_HEAD_DIM)
    cos, sin = jnp.cos(ang), jnp.sin(ang)
    cd_ref[...] = cos
    sd_ref[...] = jnp.where(lane < DIL_HEAD_DIM // 2, -sin, sin)


def _rope_tables(positions):
    batch, seq = positions.shape
    ts = 512
    m = batch * seq
    table = jax.ShapeDtypeStruct((m, LANES), F32)
    spec = pl.BlockSpec((ts, LANES), lambda i: (i, 0))
    return pl.pallas_call(
        _rope_tables_kernel,
        grid=(m // ts,),
        in_specs=[pl.BlockSpec((ts, 1), lambda i: (i, 0))],
        out_specs=[spec] * 5,
        out_shape=[table] * 5,
        compiler_params=_params("parallel"),
        name="rope_tables",
    )(positions.reshape(m, 1))


def _norm_mod_kernel(x_ref, g_ref, sc_ref, sh_ref, o_ref):
    y = _rms(x_ref[...], g_ref[...])
    o_ref[...] = (y * (1.0 + sc_ref[...]) + sh_ref[...]).astype(o_ref.dtype)


def _norm_mod(x, gain, mod, layer, shift_idx, scale_idx, batch, seq):
    m, d = x.shape
    tm = 512
    per_batch = seq // tm

    def mod_spec(which):
        return pl.BlockSpec((None, 1, d), lambda i: (layer * batch + i // per_batch, 0, which))

    return pl.pallas_call(
        _norm_mod_kernel,
        grid=(m // tm,),
        in_specs=[
            pl.BlockSpec((tm, d), lambda i: (i, 0)),
            pl.BlockSpec((None, 1, d), lambda i: (layer, 0, 0)),
            mod_spec(scale_idx),
            mod_spec(shift_idx),
        ],
        out_specs=pl.BlockSpec((tm, d), lambda i: (i, 0)),
        out_shape=jax.ShapeDtypeStruct((m, d), BF16),
        compiler_params=_params("parallel"),
        name="norm_mod",
    )(x, gain, mod, mod)


def _mm_kernel(x_ref, w_ref, o_ref):
    o_ref[...] = _dot(x_ref[...], w_ref[...].astype(BF16)).astype(o_ref.dtype)


def _row_block_spec(tm, k, index_map, single_buffer):
    if single_buffer:
        return pl.BlockSpec((tm, k), index_map, pipeline_mode=pl.Buffered(1))
    return pl.BlockSpec((tm, k), index_map)


def _matmul(x, w, layer, *, tm, tn, out_dtype, col0=0, single_buffer_x=False):
    m, k = x.shape
    n = w.shape[-1] - col0
    col_block0 = col0 // tn
    return pl.pallas_call(
        _mm_kernel,
        grid=(m // tm, n // tn),
        in_specs=[
            _row_block_spec(tm, k, lambda i, j: (i, 0), single_buffer_x),
            pl.BlockSpec((None, k, tn), lambda i, j: (layer, 0, col_block0 + j)),
        ],
        out_specs=pl.BlockSpec((tm, tn), lambda i, j: (i, j)),
        out_shape=jax.ShapeDtypeStruct((m, n), out_dtype),
        compiler_params=_params("parallel", "arbitrary"),
        name="matmul",
    )(x, w)


def _mm_resid_kernel(x_ref, w_ref, r_ref, gt_ref, o_ref):
    o_ref[...] = r_ref[...] + gt_ref[...] * _dot(x_ref[...], w_ref[...])


def _mm_resid_ksplit_kernel(x_ref, w_ref, r_ref, gt_ref, o_ref):
    @pl.when(pl.program_id(2) == 0)
    def _():
        o_ref[...] = r_ref[...] + gt_ref[...] * _dot(x_ref[...], w_ref[...])

    @pl.when(pl.program_id(2) > 0)
    def _():
        o_ref[...] += gt_ref[...] * _dot(x_ref[...], w_ref[...])


def _matmul_resid(x, w, layer, resid, mod, mod_layer, gate_idx, batch, seq, *, tm, tn, tk=None):
    m, k = x.shape
    n = w.shape[-1]
    per_batch = seq // tm
    gate_blocks = n // tn
    if tk is None:
        return pl.pallas_call(
            _mm_resid_kernel,
            grid=(m // tm, n // tn),
            in_specs=[
                pl.BlockSpec((tm, k), lambda i, j: (i, 0)),
                pl.BlockSpec((None, k, tn), lambda i, j: (layer, 0, j)),
                pl.BlockSpec((tm, tn), lambda i, j: (i, j)),
                pl.BlockSpec((None, 1, tn), lambda i, j: (mod_layer * batch + i // per_batch, 0,
                                                          gate_idx * gate_blocks + j)),
            ],
            out_specs=pl.BlockSpec((tm, tn), lambda i, j: (i, j)),
            out_shape=jax.ShapeDtypeStruct((m, n), F32),
            compiler_params=_params("parallel", "arbitrary"),
            name="matmul_resid",
        )(x, w, resid, mod)
    return pl.pallas_call(
        _mm_resid_ksplit_kernel,
        grid=(m // tm, n // tn, k // tk),
        in_specs=[
            pl.BlockSpec((tm, tk), lambda i, j, kk: (i, kk)),
            pl.BlockSpec((None, tk, tn), lambda i, j, kk: (layer, kk, j)),
            pl.BlockSpec((tm, tn), lambda i, j, kk: (i, j)),
            pl.BlockSpec((None, 1, tn), lambda i, j, kk: (mod_layer * batch + i // per_batch, 0,
                                                          gate_idx * gate_blocks + j)),
        ],
        out_specs=pl.BlockSpec((tm, tn), lambda i, j, kk: (i, j)),
        out_shape=jax.ShapeDtypeStruct((m, n), F32),
        compiler_params=_params("parallel", "parallel", "arbitrary"),
        name="matmul_resid_ksplit",
    )(x, w, resid, mod)


def _swiglu_kernel(n_src_blocks, x_ref, wg_ref, wu_ref, wd_ref, o_ref, wd_out_ref):
    first_sweep = pl.program_id(0) == 0
    in_range = pl.program_id(1) < n_src_blocks

    @pl.when(in_range)
    def _():
        x = x_ref[...]
        g = _dot(x, wg_ref[...].astype(BF16))
        u = _dot(x, wu_ref[...].astype(BF16))
        o_ref[...] = (g * jax.nn.sigmoid(g) * u).astype(o_ref.dtype)

    @pl.when(jnp.logical_not(in_range))
    def _():
        o_ref[...] = jnp.zeros_like(o_ref)

    @pl.when(first_sweep & in_range)
    def _():
        wd_out_ref[...] = wd_ref[...].astype(wd_out_ref.dtype)

    @pl.when(first_sweep & jnp.logical_not(in_range))
    def _():
        wd_out_ref[...] = jnp.zeros_like(wd_out_ref)


def _swiglu_up(x, wg, wu, wd, layer, n_out, *, tm, tn):
    m, k = x.shape
    d = wd.shape[-1]
    n_src_blocks = wg.shape[-1] // tn
    n_dst_blocks = n_out // tn
    w_spec = pl.BlockSpec((None, k, tn), lambda i, j: (layer, 0, jnp.minimum(j, n_src_blocks - 1)))
    wd_in = pl.BlockSpec((None, tn, d), lambda i, j: (
        layer, jnp.where(i == 0, jnp.minimum(j, n_src_blocks - 1), n_src_blocks - 1), 0))
    wd_out = pl.BlockSpec((tn, d), lambda i, j: (jnp.where(i == 0, j, n_dst_blocks - 1), 0))
    return pl.pallas_call(
        functools.partial(_swiglu_kernel, n_src_blocks),
        grid=(m // tm, n_dst_blocks),
        in_specs=[_row_block_spec(tm, k, lambda i, j: (i, 0), True), w_spec, w_spec, wd_in],
        out_specs=[pl.BlockSpec((tm, tn), lambda i, j: (i, j)), wd_out],
        out_shape=[jax.ShapeDtypeStruct((m, n_out), BF16), jax.ShapeDtypeStruct((n_out, d), BF16)],
        compiler_params=_params("arbitrary", "arbitrary"),
        name="swiglu_up",
    )(x, wg, wu, wd)


def _rope_mla(x, cos, s1, s2):
    half = MLA_ROPE_DIM // 2
    return (x * cos + pltpu.roll(x, LANES - half, 1) * s1 + pltpu.roll(x, half, 1) * s2)


def _rope_dil(x, cos, sin_signed):
    return x * cos + pltpu.roll(x, DIL_HEAD_DIM // 2, 1) * sin_signed


def _deferred_schedule(n_i, n_j):
    steps = n_i * n_j

    def cur(t):
        t = jnp.minimum(t, steps - 1)
        return t // n_j, t % n_j

    def prev(t):
        t = jnp.maximum(t - 1, 0)
        return t // n_j, t % n_j

    return steps, cur, prev


def _deferred_prologue(n_j, acc_ref):
    t = pl.program_id(0)

    @pl.when(t == 0)
    def _():
        acc_ref[...] = jnp.zeros_like(acc_ref)

    return lax.rem(t, n_j) == 0


def _mla_q_nope_kernel(n_j, cq_ref, gqa_ref, w_ref, gn_ref, o_ref, hs_ref, acc_ref):
    @pl.when(_deferred_prologue(n_j, acc_ref))
    def _():
        hs_ref[...] = _rms(cq_ref[...], gqa_ref[...]).astype(hs_ref.dtype)

    for t in range(acc_ref.shape[1] // LANES):
        cols = slice(t * LANES, (t + 1) * LANES)
        o_ref[:, cols] = _rms(acc_ref[:, cols], gn_ref[...]).astype(o_ref.dtype)
    acc_ref[...] = _dot(hs_ref[...], w_ref[...])


def _mla_q_pe_kernel(n_j, cq_ref, gqa_ref, w_ref, gp_ref, cos_ref, s1_ref, s2_ref, o_ref, hs_ref, acc_ref):
    @pl.when(_deferred_prologue(n_j, acc_ref))
    def _():
        hs_ref[...] = _rms(cq_ref[...], gqa_ref[...]).astype(hs_ref.dtype)

    cos, s1, s2 = cos_ref[...], s1_ref[...], s2_ref[...]
    low = lax.broadcasted_iota(jnp.int32, (1, LANES), 1) < MLA_ROPE_DIM
    for t in range(acc_ref.shape[1] // LANES):
        cols = slice(t * LANES, (t + 1) * LANES)
        x = acc_ref[:, cols]
        sq = x * x
        ss_low = jnp.sum(jnp.where(low, sq, 0.0), axis=-1, keepdims=True)
        ss_high = jnp.sum(jnp.where(low, 0.0, sq), axis=-1, keepdims=True)
        ms = jnp.where(low, ss_low, ss_high) * (1.0 / MLA_ROPE_DIM)
        y = x * lax.rsqrt(ms + NORM_EPS) * gp_ref[...]
        o_ref[:, cols] = _rope_mla(y, cos, s1, s2).astype(o_ref.dtype)
    acc_ref[...] = _dot(hs_ref[...], w_ref[...])


def _mla_q_proj(a, g_q_a, w, layer, g_nope, g_pe_pair, tables, *, tm, tn):
    m = a.shape[0]
    k = w.shape[1]
    n_nope = MLA_HEADS * MLA_NOPE_DIM
    n_pe = MLA_HEADS * MLA_ROPE_DIM
    cos, s1, s2 = tables
    gain = pl.BlockSpec((None, 1, LANES), lambda t: (layer, 0, 0))

    def call(body, name, n_out, col_block0, with_tables, extra_args):
        n_j = n_out // tn
        steps, cur, prev = _deferred_schedule(m // tm, n_j)
        tab = pl.BlockSpec((tm, LANES), lambda t: (prev(t)[0], 0))
        return pl.pallas_call(
            functools.partial(body, n_j),
            grid=(steps + 1,),
            in_specs=[
                pl.BlockSpec((tm, k), lambda t: (cur(t)[0], 0)),
                pl.BlockSpec((None, 1, k), lambda t: (layer, 0, 0)),
                pl.BlockSpec((None, k, tn), lambda t: (layer, 0, col_block0 + cur(t)[1])),
                gain,
            ] + ([tab, tab, tab] if with_tables else []),
            out_specs=pl.BlockSpec((tm, tn), lambda t: prev(t)),
            out_shape=jax.ShapeDtypeStruct((m, n_out), BF16),
            scratch_shapes=[pltpu.VMEM((tm, k), BF16), pltpu.VMEM((tm, tn), F32)],
            compiler_params=_params("arbitrary"),
            name=name,
        )(a, g_q_a, w, *extra_args)

    q_nope = call(_mla_q_nope_kernel, "mla_q_nope_proj", n_nope, 0, False, [g_nope])
    q_pe = call(_mla_q_pe_kernel, "mla_q_pe_proj", n_pe, n_nope // tn, True, [g_pe_pair, cos, s1, s2])
    return q_nope, q_pe


MLA_KV_BLOCK = 768


def _mla_kv_kernel(ckv_ref, gkva_ref, w_ref, gk_ref, gp_ref, cos_ref, s1_ref, s2_ref,
                   kv_ref, kpe_ref, hs_ref):
    @pl.when(pl.program_id(1) == 0)
    def _():
        blk = ckv_ref[...]
        hs_ref[...] = _rms(blk[:, :MLA_KV_RANK], gkva_ref[...]).astype(hs_ref.dtype)
        pe = _rms(blk[:, MLA_KV_RANK:MLA_KV_RANK + LANES], gp_ref[...], inv_n=1.0 / MLA_ROPE_DIM)
        pe = _rope_mla(pe, cos_ref[...], s1_ref[...], s2_ref[...])
        kpe_ref[:, :LANES] = pe.astype(kpe_ref.dtype)
        kpe_ref[:, LANES:] = pltpu.roll(pe, MLA_ROPE_DIM, 1).astype(kpe_ref.dtype)

    acc = _dot(hs_ref[...], w_ref[...])
    head_w = MLA_NOPE_DIM + MLA_V_DIM
    for hh in range(acc.shape[1] // head_w):
        k_cols = slice(hh * head_w, hh * head_w + MLA_NOPE_DIM)
        v_cols = slice(hh * head_w + MLA_NOPE_DIM, (hh + 1) * head_w)
        kv_ref[:, k_cols] = _rms(acc[:, k_cols], gk_ref[...]).astype(kv_ref.dtype)
        kv_ref[:, v_cols] = acc[:, v_cols].astype(kv_ref.dtype)


def _mla_kv_proj(a, g_kv_a, w, layer, g_k_nope, g_pe_pad, tables, *, tm, tn):
    m = a.shape[0]
    k, n = w.shape[1:]
    cos, s1, s2 = tables
    tab = pl.BlockSpec((tm, LANES), lambda i, j: (i, 0))
    gain = pl.BlockSpec((None, 1, LANES), lambda i, j: (layer, 0, 0))
    kv_col_block = MLA_Q_RANK // MLA_KV_BLOCK
    return pl.pallas_call(
        _mla_kv_kernel,
        grid=(m // tm, n // tn),
        in_specs=[
            pl.BlockSpec((tm, MLA_KV_BLOCK), lambda i, j: (i, kv_col_block)),
            pl.BlockSpec((None, 1, k), lambda i, j: (layer, 0, 0)),
            pl.BlockSpec((None, k, tn), lambda i, j: (layer, 0, j)),
            gain, gain,
            tab, tab, tab,
        ],
        out_specs=[
            pl.BlockSpec((tm, tn), lambda i, j: (i, j)),
            pl.BlockSpec((tm, 2 * LANES), lambda i, j: (i, 0)),
        ],
        out_shape=[
            jax.ShapeDtypeStruct((m, n), BF16),
            jax.ShapeDtypeStruct((m, 2 * LANES), BF16),
        ],
        scratch_shapes=[pltpu.VMEM((tm, k), BF16)],
        compiler_params=_params("parallel", "arbitrary"),
        name="mla_kv_proj",
    )(a, g_kv_a, w, g_k_nope, g_pe_pad, cos, s1, s2)


def _softmax_pv(scores, values, scale):
    c = scale * LOG2_E
    m = None
    for s in scores:
        mi = jnp.max(s, axis=-1, keepdims=True)
        m = mi if m is None else jnp.maximum(m, mi)
    mc = m * c
    den = None
    acc = None
    for s, v in zip(scores, values):
        p = jnp.exp2(s * c - mc)
        li = jnp.sum(p, axis=-1, keepdims=True)
        ai = _dot(p.astype(v.dtype), v)
        den = li if den is None else den + li
        acc = ai if acc is None else acc + ai
    return acc / den


MLA_Q_BLOCK = 512


def _mla_attn_kernel(qn_ref, qpe_ref, kv_ref, kpe_ref, wo_ref, o_ref, wo_out_ref, kcat_ref):
    wo_out_ref[...] = wo_ref[...].astype(wo_out_ref.dtype)
    seq = qn_ref.shape[0]
    tq = MLA_Q_BLOCK
    scale = (MLA_NOPE_DIM + MLA_ROPE_DIM) ** -0.5
    head_w = MLA_NOPE_DIM + MLA_V_DIM
    for hh in range(2):
        kcat_ref[hh, :, :MLA_NOPE_DIM] = kv_ref[:, hh * head_w:hh * head_w + MLA_NOPE_DIM]
        kcat_ref[hh, :, MLA_NOPE_DIM:] = kpe_ref[:, hh * LANES:(hh + 1) * LANES]

    row = lax.broadcasted_iota(jnp.int32, (tq, tq), 0)
    col = lax.broadcasted_iota(jnp.int32, (tq, tq), 1)
    causal = col <= row
    def block_scores(qb, hh):
        lo = qb * tq
        q = jnp.concatenate([qn_ref[lo:lo + tq, hh * LANES:(hh + 1) * LANES], qpe_ref[lo:lo + tq, :]], axis=1)
        v0 = hh * head_w + MLA_NOPE_DIM
        scores = [jnp.where(causal, _dot_nt(q, kcat_ref[hh, lo:lo + tq, :]), -jnp.inf)]
        values = [kv_ref[lo:lo + tq, v0:v0 + MLA_V_DIM]]
        if qb > 0:
            scores.append(_dot_nt(q, kcat_ref[hh, 0:lo, :]))
            values.append(kv_ref[0:lo, v0:v0 + MLA_V_DIM])
        return scores, values

    order = [(qb, hh) for qb in range(seq // tq) for hh in range(2)]
    pending = block_scores(*order[0])
    for idx, (qb, hh) in enumerate(order):
        upcoming = block_scores(*order[idx + 1]) if idx + 1 < len(order) else None
        o = _softmax_pv(*pending, scale)
        o_ref[qb * tq:(qb + 1) * tq, hh * MLA_V_DIM:(hh + 1) * MLA_V_DIM] = o.astype(o_ref.dtype)
        pending = upcoming


def _mla_attention(q_nope, q_pe, kv, kpe, w_o, layer, batch, seq):
    pairs = MLA_HEADS // 2
    wo_rows, wo_cols = w_o.shape[1:]
    slab = wo_rows // (batch * pairs)
    assert slab * batch * pairs == wo_rows
    return pl.pallas_call(
        _mla_attn_kernel,
        grid=(batch, pairs),
        in_specs=[
            pl.BlockSpec((seq, 2 * MLA_NOPE_DIM), lambda b, p: (b, p)),
            pl.BlockSpec((seq, LANES), lambda b, p: (b, p)),
            pl.BlockSpec((seq, 2 * (MLA_NOPE_DIM + MLA_V_DIM)), lambda b, p: (b, p)),
            pl.BlockSpec((seq, 2 * LANES), lambda b, p: (b, 0)),
            pl.BlockSpec((None, slab, wo_cols), lambda b, p: (layer, b * pairs + p, 0)),
        ],
        out_specs=[
            pl.BlockSpec((seq, 2 * MLA_V_DIM), lambda b, p: (b, p)),
            pl.BlockSpec((slab, wo_cols), lambda b, p: (b * pairs + p, 0)),
        ],
        out_shape=[
            jax.ShapeDtypeStruct((batch * seq, MLA_HEADS * MLA_V_DIM), BF16),
            jax.ShapeDtypeStruct((wo_rows, wo_cols), BF16),
        ],
        scratch_shapes=[pltpu.VMEM((2, seq, MLA_NOPE_DIM + LANES), BF16)],
        compiler_params=_params("parallel", "parallel"),
        name="mla_attention",
    )(q_nope, q_pe, kv, kpe, w_o)


def _dil_qk_kernel(n_j, x_ref, w_ref, gain_ref, cos_ref, sin_ref, o_ref, acc_ref):
    _deferred_prologue(n_j, acc_ref)
    for r0 in range(0, acc_ref.shape[0], EPILOGUE_ROWS):
        rows = slice(r0, r0 + EPILOGUE_ROWS)
        cos, sin = cos_ref[rows, :], sin_ref[rows, :]
        for hh in range(acc_ref.shape[1] // DIL_HEAD_DIM):
            cols = slice(hh * DIL_HEAD_DIM, (hh + 1) * DIL_HEAD_DIM)
            y = _rms(acc_ref[rows, cols], gain_ref[:, cols])
            o_ref[rows, cols] = _rope_dil(y, cos, sin).astype(o_ref.dtype)
    acc_ref[...] = _dot(x_ref[...], w_ref[...].astype(BF16))


def _dil_qk_proj(x, w, layer, gain_cols, tables, *, tm, tn):
    m, k = x.shape
    n = gain_cols.shape[-1]
    cos, sin = tables
    n_j = n // tn
    steps, cur, prev = _deferred_schedule(m // tm, n_j)
    tab = pl.BlockSpec((tm, LANES), lambda t: (prev(t)[0], 0))
    return pl.pallas_call(
        functools.partial(_dil_qk_kernel, n_j),
        grid=(steps + 1,),
        in_specs=[
            _row_block_spec(tm, k, lambda t: (cur(t)[0], 0), True),
            pl.BlockSpec((None, k, tn), lambda t: (layer, 0, cur(t)[1])),
            pl.BlockSpec((None, 1, tn), lambda t: (layer, 0, prev(t)[1])),
            tab, tab,
        ],
        out_specs=pl.BlockSpec((tm, tn), lambda t: prev(t)),
        out_shape=jax.ShapeDtypeStruct((m, n), BF16),
        scratch_shapes=[pltpu.VMEM((tm, tn), F32)],
        compiler_params=_params("arbitrary"),
        name="dil_qk_proj",
    )(x, w, gain_cols, cos, sin)


DIL_Q_BLOCK = 128


def _dil_bias(rows, cols, offset, window, dilation):
    dist = (offset + lax.broadcasted_iota(jnp.int32, (rows, cols), 0)
            - lax.broadcasted_iota(jnp.int32, (rows, cols), 1))
    valid = (dist >= 0) & (dist <= window) & ((dist & (dilation - 1)) == 0)
    return jnp.where(valid, 0.0, -jnp.inf).astype(F32)


def _dil_attn_kernel(*refs):
    n_groups = len(DIL_GROUPS)
    q_refs = refs[:n_groups]
    k_refs = refs[n_groups:2 * n_groups]
    v_refs = refs[2 * n_groups:3 * n_groups]
    wo_ref, o_ref, wo_out_ref = refs[3 * n_groups:]
    wo_out_ref[...] = wo_ref[...].astype(wo_out_ref.dtype)
    seq = o_ref.shape[0]
    tq = DIL_Q_BLOCK
    scale = DIL_HEAD_DIM ** -0.5
    diag, full, cut = [], [], []
    for window, dilation in DIL_GROUPS:
        assert tq % dilation == 0 and (window % tq == 0 or tq % window == 0)
        diag.append(_dil_bias(tq, tq, 0, window, dilation))
        full.append(_dil_bias(tq, window, window, window, dilation) if window < seq else None)
        n_cut = min(window, seq) - tq
        cut.append(_dil_bias(tq, n_cut, n_cut, seq, dilation) if n_cut > 0 and dilation > 1 else None)

    def block_scores(t0):
        scores, values = [], []
        for g, (window, dilation) in enumerate(DIL_GROUPS):
            q = q_refs[g][t0:t0 + tq, :]
            scores.append(_dot_nt(q, k_refs[g][t0:t0 + tq, :]) + diag[g])
            values.append(v_refs[g][t0:t0 + tq, :])
            n_prev = min(window, t0)
            if n_prev == 0:
                continue
            s_prev = _dot_nt(q, k_refs[g][t0 - n_prev:t0, :])
            if n_prev == window:
                s_prev = s_prev + full[g]
            elif cut[g] is not None:
                s_prev = s_prev + cut[g][:, cut[g].shape[1] - n_prev:]
            scores.append(s_prev)
            values.append(v_refs[g][t0 - n_prev:t0, :])
        return scores, values

    n_blocks = seq // tq
    pending = block_scores(0)
    for sb in range(n_blocks):
        upcoming = block_scores((sb + 1) * tq) if sb + 1 < n_blocks else None
        o = _softmax_pv(*pending, scale)
        o_ref[sb * tq:(sb + 1) * tq, :] = o.astype(o_ref.dtype)
        pending = upcoming


def _dil_attention(qk, v, w_o, layer, batch, seq):
    n_groups = len(DIL_GROUPS)
    hd = DIL_HEAD_DIM
    wo_rows, wo_cols = w_o.shape[1:]
    slab = wo_rows // (batch * DIL_HEADS)
    assert slab * batch * DIL_HEADS == wo_rows

    def spec(part, g):
        c = (part * n_groups + g) * DIL_HEADS
        return pl.BlockSpec((seq, hd), lambda b, h: (b, c + h))

    return pl.pallas_call(
        _dil_attn_kernel,
        grid=(batch, DIL_HEADS),
        in_specs=[spec(part, g) for part in range(2) for g in range(n_groups)]
        + [spec(0, g) for g in range(n_groups)]
        + [pl.BlockSpec((None, slab, wo_cols), lambda b, h: (layer, b * DIL_HEADS + h, 0))],
        out_specs=[
            pl.BlockSpec((seq, hd), lambda b, h: (b, h)),
            pl.BlockSpec((slab, wo_cols), lambda b, h: (b * DIL_HEADS + h, 0)),
        ],
        out_shape=[
            jax.ShapeDtypeStruct((batch * seq, DIL_HEADS * hd), BF16),
            jax.ShapeDtypeStruct((wo_rows, wo_cols), BF16),
        ],
        compiler_params=_params("parallel", "parallel"),
        name="dil_attention",
    )(*([qk] * (2 * n_groups) + [v] * n_groups + [w_o]))


def _pad_last(x, width):
    return jnp.pad(x, [(0, 0)] * (x.ndim - 1) + [(0, width - x.shape[-1])])


def kernel(x, c, positions, w_cond, b_cond, w_mod, b_mod, g_mix_norm, g_ffn_norm, mla_w_in, mla_g_q_a, mla_g_kv_a, mla_w_q_b, mla_w_kv_b, mla_g_q_nope, mla_g_q_pe, mla_g_k_nope, mla_g_k_pe, mla_w_o, dil_w_qkv, dil_g_q, dil_g_k, dil_w_o, ffn_w_gate, ffn_w_up, ffn_w_down):
    batch, seq, d = x.shape
    depth = w_mod.shape[0]
    n_a = mla_w_in.shape[0]
    n_b = dil_w_qkv.shape[0]
    m = batch * seq
    hidden = ffn_w_gate.shape[-1]
    hidden_pad = -(-hidden // FFN_PAD_MULTIPLE) * FFN_PAD_MULTIPLE

    w_in = _pad_last(mla_w_in.astype(BF16), MLA_IN_PAD)
    qk_dim = MLA_NOPE_DIM + MLA_ROPE_DIM
    w_q_b = mla_w_q_b.astype(BF16).reshape(n_a, MLA_Q_RANK, MLA_HEADS, qk_dim)
    w_q_b = jnp.concatenate(
        [w_q_b[..., :MLA_NOPE_DIM].reshape(n_a, MLA_Q_RANK, MLA_HEADS * MLA_NOPE_DIM),
         w_q_b[..., MLA_NOPE_DIM:].reshape(n_a, MLA_Q_RANK, MLA_HEADS * MLA_ROPE_DIM)], axis=-1)
    w_kv_b = mla_w_kv_b.astype(BF16)

    g_mix = g_mix_norm[:, None, :]
    g_ffn = g_ffn_norm[:, None, :]
    g_q_a = mla_g_q_a[:, None, :]
    g_kv_a = mla_g_kv_a[:, None, :]
    g_q_nope = mla_g_q_nope[:, None, :]
    g_k_nope = mla_g_k_nope[:, None, :]
    g_q_pe = jnp.concatenate([mla_g_q_pe, mla_g_q_pe], axis=-1)[:, None, :]
    g_k_pe = _pad_last(mla_g_k_pe, LANES)[:, None, :]
    n_groups = len(DIL_GROUPS)
    n_qk = 2 * n_groups * DIL_HEADS * DIL_HEAD_DIM
    dil_gain = jnp.stack([dil_g_q, dil_g_k], axis=1)
    dil_gain = jnp.broadcast_to(dil_gain[:, :, :, None, :], (n_b, 2, n_groups, DIL_HEADS, DIL_HEAD_DIM))
    dil_gain = dil_gain.reshape(n_b, 1, n_qk)

    mod = _cond_mod(c, w_cond, b_cond, w_mod, b_mod)
    cos_m, s1_m, s2_m, cos_d, sin_d = _rope_tables(positions)

    xr = x.reshape(m, d)
    for i in range(depth):
        j = i // 2
        h = _norm_mod(xr, g_mix, mod, i, 0, 1, batch, seq)
        if i % 2 == 0:
            a = _matmul(h, w_in, j, tm=1024, tn=MLA_KV_BLOCK, out_dtype=F32)
            q_nope, q_pe = _mla_q_proj(a, g_q_a, w_q_b, j, g_q_nope, g_q_pe, (cos_m, s1_m, s2_m),
                                       tm=1024, tn=2048)
            kv, kpe = _mla_kv_proj(a, g_kv_a, w_kv_b, j, g_k_nope, g_k_pe, (cos_m, s1_m, s2_m),
                                   tm=1024, tn=2048)
            o, w_o = _mla_attention(q_nope, q_pe, kv, kpe, mla_w_o, j, batch, seq)
            xr = _matmul_resid(o, w_o[None], 0, xr, mod, i, 2, batch, seq, tm=1024, tn=512)
        else:
            qk = _dil_qk_proj(h, dil_w_qkv, j, dil_gain, (cos_d, sin_d), tm=2048, tn=512)
            v = _matmul(h, dil_w_qkv, j, tm=2048, tn=512, out_dtype=BF16, col0=n_qk,
                        single_buffer_x=True)
            o, w_o = _dil_attention(qk, v, dil_w_o, j, batch, seq)
            xr = _matmul_resid(o, w_o[None], 0, xr, mod, i, 2, batch, seq, tm=1024, tn=1024)
        h = _norm_mod(xr, g_ffn, mod, i, 3, 4, batch, seq)
        act, w_down = _swiglu_up(h, ffn_w_gate, ffn_w_up, ffn_w_down, i, hidden_pad,
                                 tm=2048, tn=FFN_COL_BLOCK)
        xr = _matmul_resid(act, w_down[None], 0, xr, mod, i, 5, batch, seq, tm=1024, tn=1024,
                           tk=hidden_pad // 4)
    return xr.reshape(batch, seq, d)
```

```python
import functools
import math

import jax
import jax.numpy as jnp
from jax import lax
from jax.experimental import pallas as pl
from jax.experimental.pallas import tpu as pltpu

F32 = jnp.float32
BF16 = jnp.bfloat16

ROPE_THETA = 10000.0
NORM_EPS = 1e-6
N_MOD = 6

MLA_HEADS = 32
MLA_Q_RANK = 1536
MLA_KV_RANK = 512
MLA_NOPE_DIM = 128
MLA_ROPE_DIM = 64
MLA_V_DIM = 128

DIL_GROUPS = ((128, 1), (512, 4), (2048, 16))
DIL_HEADS = 16
DIL_HEAD_DIM = 128

LANES = 128
V7X_VMEM_LIMIT_BYTES = 56 * 2**20

FFN_PAD_MULTIPLE = 1024
EPILOGUE_ROWS = 256
FFN_COL_BLOCK = 256
MLA_IN_PAD = 2304
LOG2_E = math.log2(math.e)


def _params(*semantics):
    return pltpu.CompilerParams(dimension_semantics=semantics,
                                vmem_limit_bytes=V7X_VMEM_LIMIT_BYTES)


def _dot(a, b):
    return jnp.dot(a, b, preferred_element_type=F32)


def _dot_nt(a, b):
    return lax.dot_general(a, b, (((1,), (1,)), ((), ())), preferred_element_type=F32)


def _rms(x, gain, inv_n=None):
    if inv_n is None:
        ms = jnp.mean(x * x, axis=-1, keepdims=True)
    else:
        ms = jnp.sum(x * x, axis=-1, keepdims=True) * inv_n
    return x * lax.rsqrt(ms + NORM_EPS) * gain


def _cond_mod_kernel(c_ref, wc_ref, bc_ref, wm_ref, bm_ref, o_ref, e_ref):
    @pl.when((pl.program_id(0) == 0) & (pl.program_id(1) == 0))
    def _():
        z = _dot(c_ref[...].astype(BF16), wc_ref[...].astype(BF16)) + bc_ref[...]
        e_ref[...] = z * jax.nn.sigmoid(z)

    o_ref[...] = _dot(e_ref[...].astype(BF16), wm_ref[...].astype(BF16)) + bm_ref[...]


def _cond_mod(c, w_cond, b_cond, w_mod, b_mod):
    batch, d = c.shape
    depth, rank, n = w_mod.shape
    rows = 8
    tn = 2048
    c8 = jnp.pad(c, ((0, rows - batch), (0, 0)))
    out = pl.pallas_call(
        _cond_mod_kernel,
        grid=(depth, n // tn),
        in_specs=[
            pl.BlockSpec((rows, d), lambda l, j: (0, 0)),
            pl.BlockSpec((d, rank), lambda l, j: (0, 0)),
            pl.BlockSpec((1, rank), lambda l, j: (0, 0)),
            pl.BlockSpec((None, rank, tn), lambda l, j: (l, 0, j)),
            pl.BlockSpec((None, 1, tn), lambda l, j: (l, 0, j)),
        ],
        out_specs=pl.BlockSpec((None, rows, tn), lambda l, j: (l, 0, j)),
        out_shape=jax.ShapeDtypeStruct((depth, rows, n), F32),
        scratch_shapes=[pltpu.VMEM((rows, rank), F32)],
        compiler_params=_params("arbitrary", "arbitrary"),
        name="cond_mod",
    )(c8, w_cond, b_cond.reshape(1, rank), w_mod, b_mod.reshape(depth, 1, n))
    return out[:, :batch].reshape(depth * batch, 1, n)


def _rope_tables_kernel(pos_ref, cm_ref, s1_ref, s2_ref, cd_ref, sd_ref):
    pos = pos_ref[...].astype(F32)
    lane = lax.broadcasted_iota(jnp.int32, (1, LANES), 1)

    def angles(dim):
        half = dim // 2
        idx = (lane & (half - 1)).astype(F32)
        inv_freq = jnp.exp(idx * (-2.0 * math.log(ROPE_THETA) / dim))
        return pos * inv_freq

    ang = angles(MLA_ROPE_DIM)
    cos, sin = jnp.cos(ang), jnp.sin(ang)
    first_half = (lane & (MLA_ROPE_DIM - 1)) < MLA_ROPE_DIM // 2
    cm_ref[...] = cos
    s1_ref[...] = jnp.where(first_half, -sin, 0.0)
    s2_ref[...] = jnp.where(first_half, 0.0, sin)
    ang = angles(DIL_HEAD_DIM)
    cos, sin = jnp.cos(ang), jnp.sin(ang)
    cd_ref[...] = cos
    sd_ref[...] = jnp.where(lane < DIL_HEAD_DIM // 2, -sin, sin)


def _rope_tables(positions):
    batch, seq = positions.shape
    ts = 512
    m = batch * seq
    table = jax.ShapeDtypeStruct((m, LANES), F32)
    spec = pl.BlockSpec((ts, LANES), lambda i: (i, 0))
    return pl.pallas_call(
        _rope_tables_kernel,
        grid=(m // ts,),
        in_specs=[pl.BlockSpec((ts, 1), lambda i: (i, 0))],
        out_specs=[spec] * 5,
        out_shape=[table] * 5,
        compiler_params=_params("parallel"),
        name="rope_tables",
    )(positions.reshape(m, 1))


def _norm_mod_kernel(x_ref, g_ref, sc_ref, sh_ref, o_ref):
    y = _rms(x_ref[...], g_ref[...])
    o_ref[...] = (y * (1.0 + sc_ref[...]) + sh_ref[...]).astype(o_ref.dtype)


def _norm_mod(x, gain, mod, layer, shift_idx, scale_idx, batch, seq):
    m, d = x.shape
    tm = 512
    per_batch = seq // tm

    def mod_spec(which):
        return pl.BlockSpec((None, 1, d), lambda i: (layer * batch + i // per_batch, 0, which))

    return pl.pallas_call(
        _norm_mod_kernel,
        grid=(m // tm,),
        in_specs=[
            pl.BlockSpec((tm, d), lambda i: (i, 0)),
            pl.BlockSpec((None, 1, d), lambda i: (layer, 0, 0)),
            mod_spec(scale_idx),
            mod_spec(shift_idx),
        ],
        out_specs=pl.BlockSpec((tm, d), lambda i: (i, 0)),
        out_shape=jax.ShapeDtypeStruct((m, d), BF16),
        compiler_params=_params("parallel"),
        name="norm_mod",
    )(x, gain, mod, mod)


def _mm_kernel(x_ref, w_ref, o_ref):
    o_ref[...] = _dot(x_ref[...], w_ref[...].astype(BF16)).astype(o_ref.dtype)


def _resident_row_block(tm, k, index_map):
    return pl.BlockSpec((tm, k), index_map, pipeline_mode=pl.Buffered(1))


def _matmul(x, w, layer, *, tm, tn, out_dtype):
    m, k = x.shape
    n = w.shape[-1]
    return pl.pallas_call(
        _mm_kernel,
        grid=(m // tm, n // tn),
        in_specs=[
            pl.BlockSpec((tm, k), lambda i, j: (i, 0)),
            pl.BlockSpec((None, k, tn), lambda i, j: (layer, 0, j)),
        ],
        out_specs=pl.BlockSpec((tm, tn), lambda i, j: (i, j)),
        out_shape=jax.ShapeDtypeStruct((m, n), out_dtype),
        compiler_params=_params("parallel", "arbitrary"),
        name="matmul",
    )(x, w)


def _mm_resid_kernel(x_ref, w_ref, r_ref, gt_ref, o_ref):
    o_ref[...] = r_ref[...] + gt_ref[...] * _dot(x_ref[...], w_ref[...])


def _mm_resid_ksplit_kernel(x_ref, w_ref, r_ref, gt_ref, o_ref):
    @pl.when(pl.program_id(2) == 0)
    def _():
        o_ref[...] = r_ref[...] + gt_ref[...] * _dot(x_ref[...], w_ref[...])

    @pl.when(pl.program_id(2) > 0)
    def _():
        o_ref[...] += gt_ref[...] * _dot(x_ref[...], w_ref[...])


def _matmul_resid(x, w, layer, resid, mod, mod_layer, gate_idx, batch, seq, *, tm, tn, tk=None):
    m, k = x.shape
    n = w.shape[-1]
    per_batch = seq // tm
    gate_blocks = n // tn
    if tk is None:
        return pl.pallas_call(
            _mm_resid_kernel,
            grid=(m // tm, n // tn),
            in_specs=[
                pl.BlockSpec((tm, k), lambda i, j: (i, 0)),
                pl.BlockSpec((None, k, tn), lambda i, j: (layer, 0, j)),
                pl.BlockSpec((tm, tn), lambda i, j: (i, j)),
                pl.BlockSpec((None, 1, tn), lambda i, j: (mod_layer * batch + i // per_batch, 0,
                                                          gate_idx * gate_blocks + j)),
            ],
            out_specs=pl.BlockSpec((tm, tn), lambda i, j: (i, j)),
            out_shape=jax.ShapeDtypeStruct((m, n), F32),
            compiler_params=_params("parallel", "arbitrary"),
            name="matmul_resid",
        )(x, w, resid, mod)
    return pl.pallas_call(
        _mm_resid_ksplit_kernel,
        grid=(m // tm, n // tn, k // tk),
        in_specs=[
            pl.BlockSpec((tm, tk), lambda i, j, kk: (i, kk)),
            pl.BlockSpec((None, tk, tn), lambda i, j, kk: (layer, kk, j)),
            pl.BlockSpec((tm, tn), lambda i, j, kk: (i, j)),
            pl.BlockSpec((None, 1, tn), lambda i, j, kk: (mod_layer * batch + i // per_batch, 0,
                                                          gate_idx * gate_blocks + j)),
        ],
        out_specs=pl.BlockSpec((tm, tn), lambda i, j, kk: (i, j)),
        out_shape=jax.ShapeDtypeStruct((m, n), F32),
        compiler_params=_params("parallel", "parallel", "arbitrary"),
        name="matmul_resid_ksplit",
    )(x, w, resid, mod)


def _swiglu_kernel(n_src_blocks, x_ref, wg_ref, wu_ref, wd_ref, o_ref, wd_out_ref):
    first_sweep = pl.program_id(0) == 0
    in_range = pl.program_id(1) < n_src_blocks

    @pl.when(in_range)
    def _():
        x = x_ref[...]
        g = _dot(x, wg_ref[...].astype(BF16))
        u = _dot(x, wu_ref[...].astype(BF16))
        o_ref[...] = (g * jax.nn.sigmoid(g) * u).astype(o_ref.dtype)

    @pl.when(jnp.logical_not(in_range))
    def _():
        o_ref[...] = jnp.zeros_like(o_ref)

    @pl.when(first_sweep & in_range)
    def _():
        wd_out_ref[...] = wd_ref[...].astype(wd_out_ref.dtype)

    @pl.when(first_sweep & jnp.logical_not(in_range))
    def _():
        wd_out_ref[...] = jnp.zeros_like(wd_out_ref)


def _swiglu_up(x, wg, wu, wd, layer, n_out, *, tm, tn):
    m, k = x.shape
    d = wd.shape[-1]
    n_src_blocks = wg.shape[-1] // tn
    n_dst_blocks = n_out // tn
    w_spec = pl.BlockSpec((None, k, tn), lambda i, j: (layer, 0, jnp.minimum(j, n_src_blocks - 1)))
    wd_in = pl.BlockSpec((None, tn, d), lambda i, j: (
        layer, jnp.where(i == 0, jnp.minimum(j, n_src_blocks - 1), n_src_blocks - 1), 0))
    wd_out = pl.BlockSpec((tn, d), lambda i, j: (jnp.where(i == 0, j, n_dst_blocks - 1), 0))
    return pl.pallas_call(
        functools.partial(_swiglu_kernel, n_src_blocks),
        grid=(m // tm, n_dst_blocks),
        in_specs=[_resident_row_block(tm, k, lambda i, j: (i, 0)), w_spec, w_spec, wd_in],
        out_specs=[pl.BlockSpec((tm, tn), lambda i, j: (i, j)), wd_out],
        out_shape=[jax.ShapeDtypeStruct((m, n_out), BF16), jax.ShapeDtypeStruct((n_out, d), BF16)],
        compiler_params=_params("arbitrary", "arbitrary"),
        name="swiglu_up",
    )(x, wg, wu, wd)


def _rope_mla(x, cos, s1, s2):
    half = MLA_ROPE_DIM // 2
    return (x * cos + pltpu.roll(x, LANES - half, 1) * s1 + pltpu.roll(x, half, 1) * s2)


def _rope_dil(x, cos, sin_signed):
    return x * cos + pltpu.roll(x, DIL_HEAD_DIM // 2, 1) * sin_signed


def _deferred_schedule(n_i, n_j):
    steps = n_i * n_j

    def cur(t):
        t = jnp.minimum(t, steps - 1)
        return t // n_j, t % n_j

    def prev(t):
        t = jnp.maximum(t - 1, 0)
        return t // n_j, t % n_j

    return steps, cur, prev


def _deferred_prologue(n_j, acc_ref):
    t = pl.program_id(0)

    @pl.when(t == 0)
    def _():
        acc_ref[...] = jnp.zeros_like(acc_ref)

    return lax.rem(t, n_j) == 0


def _mla_q_nope_kernel(n_j, cq_ref, gqa_ref, w_ref, gn_ref, o_ref, hs_ref, acc_ref):
    @pl.when(_deferred_prologue(n_j, acc_ref))
    def _():
        hs_ref[...] = _rms(cq_ref[...], gqa_ref[...]).astype(hs_ref.dtype)

    for t in range(acc_ref.shape[1] // LANES):
        cols = slice(t * LANES, (t + 1) * LANES)
        o_ref[:, cols] = _rms(acc_ref[:, cols], gn_ref[...]).astype(o_ref.dtype)
    acc_ref[...] = _dot(hs_ref[...], w_ref[...])


def _mla_q_pe_kernel(n_j, cq_ref, gqa_ref, w_ref, gp_ref, cos_ref, s1_ref, s2_ref, o_ref, hs_ref, acc_ref):
    @pl.when(_deferred_prologue(n_j, acc_ref))
    def _():
        hs_ref[...] = _rms(cq_ref[...], gqa_ref[...]).astype(hs_ref.dtype)

    cos, s1, s2 = cos_ref[...], s1_ref[...], s2_ref[...]
    low = lax.broadcasted_iota(jnp.int32, (1, LANES), 1) < MLA_ROPE_DIM
    for t in range(acc_ref.shape[1] // LANES):
        cols = slice(t * LANES, (t + 1) * LANES)
        x = acc_ref[:, cols]
        sq = x * x
        ss_low = jnp.sum(jnp.where(low, sq, 0.0), axis=-1, keepdims=True)
        ss_high = jnp.sum(jnp.where(low, 0.0, sq), axis=-1, keepdims=True)
        ms = jnp.where(low, ss_low, ss_high) * (1.0 / MLA_ROPE_DIM)
        y = x * lax.rsqrt(ms + NORM_EPS) * gp_ref[...]
        o_ref[:, cols] = _rope_mla(y, cos, s1, s2).astype(o_ref.dtype)
    acc_ref[...] = _dot(hs_ref[...], w_ref[...])


def _mla_q_proj(a, g_q_a, w, layer, g_nope, g_pe_pair, tables, *, tm, tn):
    m = a.shape[0]
    k = w.shape[1]
    n_nope = MLA_HEADS * MLA_NOPE_DIM
    n_pe = MLA_HEADS * MLA_ROPE_DIM
    cos, s1, s2 = tables
    gain = pl.BlockSpec((None, 1, LANES), lambda t: (layer, 0, 0))

    def call(body, name, n_out, col_block0, with_tables, extra_args):
        n_j = n_out // tn
        steps, cur, prev = _deferred_schedule(m // tm, n_j)
        tab = pl.BlockSpec((tm, LANES), lambda t: (prev(t)[0], 0))
        return pl.pallas_call(
            functools.partial(body, n_j),
            grid=(steps + 1,),
            in_specs=[
                pl.BlockSpec((tm, k), lambda t: (cur(t)[0], 0)),
                pl.BlockSpec((None, 1, k), lambda t: (layer, 0, 0)),
                pl.BlockSpec((None, k, tn), lambda t: (layer, 0, col_block0 + cur(t)[1])),
                gain,
            ] + ([tab, tab, tab] if with_tables else []),
            out_specs=pl.BlockSpec((tm, tn), lambda t: prev(t)),
            out_shape=jax.ShapeDtypeStruct((m, n_out), BF16),
            scratch_shapes=[pltpu.VMEM((tm, k), BF16), pltpu.VMEM((tm, tn), F32)],
            compiler_params=_params("arbitrary"),
            name=name,
        )(a, g_q_a, w, *extra_args)

    q_nope = call(_mla_q_nope_kernel, "mla_q_nope_proj", n_nope, 0, False, [g_nope])
    q_pe = call(_mla_q_pe_kernel, "mla_q_pe_proj", n_pe, n_nope // tn, True, [g_pe_pair, cos, s1, s2])
    return q_nope, q_pe


MLA_KV_BLOCK = 768


def _mla_kv_kernel(ckv_ref, gkva_ref, w_ref, gk_ref, gp_ref, cos_ref, s1_ref, s2_ref,
                   kv_ref, kpe_ref, hs_ref):
    @pl.when(pl.program_id(1) == 0)
    def _():
        blk = ckv_ref[...]
        hs_ref[...] = _rms(blk[:, :MLA_KV_RANK], gkva_ref[...]).astype(hs_ref.dtype)
        pe = _rms(blk[:, MLA_KV_RANK:MLA_KV_RANK + LANES], gp_ref[...], inv_n=1.0 / MLA_ROPE_DIM)
        pe = _rope_mla(pe, cos_ref[...], s1_ref[...], s2_ref[...])
        kpe_ref[:, :LANES] = pe.astype(kpe_ref.dtype)
        kpe_ref[:, LANES:] = pltpu.roll(pe, MLA_ROPE_DIM, 1).astype(kpe_ref.dtype)

    acc = _dot(hs_ref[...], w_ref[...])
    head_w = MLA_NOPE_DIM + MLA_V_DIM
    for hh in range(acc.shape[1] // head_w):
        k_cols = slice(hh * head_w, hh * head_w + MLA_NOPE_DIM)
        v_cols = slice(hh * head_w + MLA_NOPE_DIM, (hh + 1) * head_w)
        kv_ref[:, k_cols] = _rms(acc[:, k_cols], gk_ref[...]).astype(kv_ref.dtype)
        kv_ref[:, v_cols] = acc[:, v_cols].astype(kv_ref.dtype)


def _mla_kv_proj(a, g_kv_a, w, layer, g_k_nope, g_pe_pad, tables, *, tm, tn):
    m = a.shape[0]
    k, n = w.shape[1:]
    cos, s1, s2 = tables
    tab = pl.BlockSpec((tm, LANES), lambda i, j: (i, 0))
    gain = pl.BlockSpec((None, 1, LANES), lambda i, j: (layer, 0, 0))
    kv_col_block = MLA_Q_RANK // MLA_KV_BLOCK
    return pl.pallas_call(
        _mla_kv_kernel,
        grid=(m // tm, n // tn),
        in_specs=[
            pl.BlockSpec((tm, MLA_KV_BLOCK), lambda i, j: (i, kv_col_block)),
            pl.BlockSpec((None, 1, k), lambda i, j: (layer, 0, 0)),
            pl.BlockSpec((None, k, tn), lambda i, j: (layer, 0, j)),
            gain, gain,
            tab, tab, tab,
        ],
        out_specs=[
            pl.BlockSpec((tm, tn), lambda i, j: (i, j)),
            pl.BlockSpec((tm, 2 * LANES), lambda i, j: (i, 0)),
        ],
        out_shape=[
            jax.ShapeDtypeStruct((m, n), BF16),
            jax.ShapeDtypeStruct((m, 2 * LANES), BF16),
        ],
        scratch_shapes=[pltpu.VMEM((tm, k), BF16)],
        compiler_params=_params("parallel", "arbitrary"),
        name="mla_kv_proj",
    )(a, g_kv_a, w, g_k_nope, g_pe_pad, cos, s1, s2)


def _softmax_pv(scores, values, scale):
    c = scale * LOG2_E
    m = None
    for s in scores:
        mi = jnp.max(s, axis=-1, keepdims=True)
        m = mi if m is None else jnp.maximum(m, mi)
    mc = m * c
    den = None
    acc = None
    for s, v in zip(scores, values):
        p = jnp.exp2(s * c - mc)
        li = jnp.sum(p, axis=-1, keepdims=True)
        ai = _dot(p.astype(v.dtype), v)
        den = li if den is None else den + li
        acc = ai if acc is None else acc + ai
    return acc / den


MLA_Q_BLOCK = 512


def _mla_attn_kernel(qn_ref, qpe_ref, kv_ref, kpe_ref, wo_ref, o_ref, wo_out_ref, kcat_ref):
    wo_out_ref[...] = wo_ref[...].astype(wo_out_ref.dtype)
    seq = qn_ref.shape[0]
    tq = MLA_Q_BLOCK
    scale = (MLA_NOPE_DIM + MLA_ROPE_DIM) ** -0.5
    head_w = MLA_NOPE_DIM + MLA_V_DIM
    for hh in range(2):
        kcat_ref[hh, :, :MLA_NOPE_DIM] = kv_ref[:, hh * head_w:hh * head_w + MLA_NOPE_DIM]
        kcat_ref[hh, :, MLA_NOPE_DIM:] = kpe_ref[:, hh * LANES:(hh + 1) * LANES]

    row = lax.broadcasted_iota(jnp.int32, (tq, tq), 0)
    col = lax.broadcasted_iota(jnp.int32, (tq, tq), 1)
    causal = col <= row
    def block_scores(qb, hh):
        lo = qb * tq
        q = jnp.concatenate([qn_ref[lo:lo + tq, hh * LANES:(hh + 1) * LANES], qpe_ref[lo:lo + tq, :]], axis=1)
        v0 = hh * head_w + MLA_NOPE_DIM
        scores = [jnp.where(causal, _dot_nt(q, kcat_ref[hh, lo:lo + tq, :]), -jnp.inf)]
        values = [kv_ref[lo:lo + tq, v0:v0 + MLA_V_DIM]]
        if qb > 0:
            scores.append(_dot_nt(q, kcat_ref[hh, 0:lo, :]))
            values.append(kv_ref[0:lo, v0:v0 + MLA_V_DIM])
        return scores, values

    order = [(qb, hh) for qb in range(seq // tq) for hh in range(2)]
    pending = block_scores(*order[0])
    for idx, (qb, hh) in enumerate(order):
        upcoming = block_scores(*order[idx + 1]) if idx + 1 < len(order) else None
        o = _softmax_pv(*pending, scale)
        o_ref[qb * tq:(qb + 1) * tq, hh * MLA_V_DIM:(hh + 1) * MLA_V_DIM] = o.astype(o_ref.dtype)
        pending = upcoming


def _mla_attention(q_nope, q_pe, kv, kpe, w_o, layer, batch, seq):
    pairs = MLA_HEADS // 2
    wo_rows, wo_cols = w_o.shape[1:]
    slab = wo_rows // (batch * pairs)
    assert slab * batch * pairs == wo_rows
    return pl.pallas_call(
        _mla_attn_kernel,
        grid=(batch, pairs),
        in_specs=[
            pl.BlockSpec((seq, 2 * MLA_NOPE_DIM), lambda b, p: (b, p)),
            pl.BlockSpec((seq, LANES), lambda b, p: (b, p)),
            pl.BlockSpec((seq, 2 * (MLA_NOPE_DIM + MLA_V_DIM)), lambda b, p: (b, p)),
            pl.BlockSpec((seq, 2 * LANES), lambda b, p: (b, 0)),
            pl.BlockSpec((None, slab, wo_cols), lambda b, p: (layer, b * pairs + p, 0)),
        ],
        out_specs=[
            pl.BlockSpec((seq, 2 * MLA_V_DIM), lambda b, p: (b, p)),
            pl.BlockSpec((slab, wo_cols), lambda b, p: (b * pairs + p, 0)),
        ],
        out_shape=[
            jax.ShapeDtypeStruct((batch * seq, MLA_HEADS * MLA_V_DIM), BF16),
            jax.ShapeDtypeStruct((wo_rows, wo_cols), BF16),
        ],
        scratch_shapes=[pltpu.VMEM((2, seq, MLA_NOPE_DIM + LANES), BF16)],
        compiler_params=_params("parallel", "parallel"),
        name="mla_attention",
    )(q_nope, q_pe, kv, kpe, w_o)


def _dil_qkv_kernel(n_j, n_qk_blocks, x_ref, w_ref, gain_ref, cos_ref, sin_ref, o_ref, acc_ref):
    _deferred_prologue(n_j, acc_ref)
    finished_col = lax.rem(jnp.maximum(pl.program_id(0) - 1, 0), n_j)
    is_v = finished_col >= n_qk_blocks
    for r0 in range(0, acc_ref.shape[0], EPILOGUE_ROWS):
        rows = slice(r0, r0 + EPILOGUE_ROWS)
        cos, sin = cos_ref[rows, :], sin_ref[rows, :]
        for hh in range(acc_ref.shape[1] // DIL_HEAD_DIM):
            cols = slice(hh * DIL_HEAD_DIM, (hh + 1) * DIL_HEAD_DIM)
            x = acc_ref[rows, cols]
            y = _rope_dil(_rms(x, gain_ref[:, cols]), cos, sin)
            o_ref[rows, cols] = jnp.where(is_v, x, y).astype(o_ref.dtype)
    acc_ref[...] = _dot(x_ref[...], w_ref[...].astype(BF16))


def _dil_qkv_proj(x, w, layer, gain_cols, tables, *, tm, tn):
    m, k = x.shape
    n = w.shape[-1]
    cos, sin = tables
    n_j = n // tn
    n_qk_blocks = gain_cols.shape[-1] // tn
    steps, cur, prev = _deferred_schedule(m // tm, n_j)
    tab = pl.BlockSpec((tm, LANES), lambda t: (prev(t)[0], 0))
    return pl.pallas_call(
        functools.partial(_dil_qkv_kernel, n_j, n_qk_blocks),
        grid=(steps + 1,),
        in_specs=[
            _resident_row_block(tm, k, lambda t: (cur(t)[0], 0)),
            pl.BlockSpec((None, k, tn), lambda t: (layer, 0, cur(t)[1])),
            pl.BlockSpec((None, 1, tn), lambda t: (layer, 0, jnp.minimum(prev(t)[1], n_qk_blocks - 1))),
            tab, tab,
        ],
        out_specs=pl.BlockSpec((tm, tn), lambda t: prev(t)),
        out_shape=jax.ShapeDtypeStruct((m, n), BF16),
        scratch_shapes=[pltpu.VMEM((tm, tn), F32)],
        compiler_params=_params("arbitrary"),
        name="dil_qkv_proj",
    )(x, w, gain_cols, cos, sin)


DIL_Q_BLOCK = 128


def _dil_bias(rows, cols, offset, window, dilation):
    dist = (offset + lax.broadcasted_iota(jnp.int32, (rows, cols), 0)
            - lax.broadcasted_iota(jnp.int32, (rows, cols), 1))
    valid = (dist >= 0) & (dist <= window) & ((dist & (dilation - 1)) == 0)
    return jnp.where(valid, 0.0, -jnp.inf).astype(F32)


def _dil_attn_kernel(*refs):
    n_groups = len(DIL_GROUPS)
    q_refs = refs[:n_groups]
    k_refs = refs[n_groups:2 * n_groups]
    v_refs = refs[2 * n_groups:3 * n_groups]
    wo_ref, o_ref, wo_out_ref = refs[3 * n_groups:]
    wo_out_ref[...] = wo_ref[...].astype(wo_out_ref.dtype)
    seq = o_ref.shape[0]
    tq = DIL_Q_BLOCK
    scale = DIL_HEAD_DIM ** -0.5
    diag, full, cut = [], [], []
    for window, dilation in DIL_GROUPS:
        assert tq % dilation == 0 and (window % tq == 0 or tq % window == 0)
        diag.append(_dil_bias(tq, tq, 0, window, dilation))
        full.append(_dil_bias(tq, window, window, window, dilation) if window < seq else None)
        n_cut = min(window, seq) - tq
        cut.append(_dil_bias(tq, n_cut, n_cut, seq, dilation) if n_cut > 0 and dilation > 1 else None)

    def block_scores(t0):
        scores, values = [], []
        for g, (window, dilation) in enumerate(DIL_GROUPS):
            q = q_refs[g][t0:t0 + tq, :]
            scores.append(_dot_nt(q, k_refs[g][t0:t0 + tq, :]) + diag[g])
            values.append(v_refs[g][t0:t0 + tq, :])
            n_prev = min(window, t0)
            if n_prev == 0:
                continue
            s_prev = _dot_nt(q, k_refs[g][t0 - n_prev:t0, :])
            if n_prev == window:
                s_prev = s_prev + full[g]
            elif cut[g] is not None:
                s_prev = s_prev + cut[g][:, cut[g].shape[1] - n_prev:]
            scores.append(s_prev)
            values.append(v_refs[g][t0 - n_prev:t0, :])
        return scores, values

    n_blocks = seq // tq
    pending = block_scores(0)
    for sb in range(n_blocks):
        upcoming = block_scores((sb + 1) * tq) if sb + 1 < n_blocks else None
        o = _softmax_pv(*pending, scale)
        o_ref[sb * tq:(sb + 1) * tq, :] = o.astype(o_ref.dtype)
        pending = upcoming


def _dil_attention(qkv, w_o, layer, batch, seq):
    n_groups = len(DIL_GROUPS)
    hd = DIL_HEAD_DIM
    wo_rows, wo_cols = w_o.shape[1:]
    slab = wo_rows // (batch * DIL_HEADS)
    assert slab * batch * DIL_HEADS == wo_rows

    def spec(part, g):
        c = (part * n_groups + g) * DIL_HEADS
        return pl.BlockSpec((seq, hd), lambda b, h: (b, c + h))

    return pl.pallas_call(
        _dil_attn_kernel,
        grid=(batch, DIL_HEADS),
        in_specs=[spec(part, g) for part in range(3) for g in range(n_groups)]
        + [pl.BlockSpec((None, slab, wo_cols), lambda b, h: (layer, b * DIL_HEADS + h, 0))],
        out_specs=[
            pl.BlockSpec((seq, hd), lambda b, h: (b, h)),
            pl.BlockSpec((slab, wo_cols), lambda b, h: (b * DIL_HEADS + h, 0)),
        ],
        out_shape=[
            jax.ShapeDtypeStruct((batch * seq, DIL_HEADS * hd), BF16),
            jax.ShapeDtypeStruct((wo_rows, wo_cols), BF16),
        ],
        compiler_params=_params("parallel", "parallel"),
        name="dil_attention",
    )(*([qkv] * (3 * n_groups) + [w_o]))


def _pad_last(x, width):
    return jnp.pad(x, [(0, 0)] * (x.ndim - 1) + [(0, width - x.shape[-1])])


def kernel(x, c, positions, w_cond, b_cond, w_mod, b_mod, g_mix_norm, g_ffn_norm, mla_w_in, mla_g_q_a, mla_g_kv_a, mla_w_q_b, mla_w_kv_b, mla_g_q_nope, mla_g_q_pe, mla_g_k_nope, mla_g_k_pe, mla_w_o, dil_w_qkv, dil_g_q, dil_g_k, dil_w_o, ffn_w_gate, ffn_w_up, ffn_w_down):
    batch, seq, d = x.shape
    depth = w_mod.shape[0]
    n_a = mla_w_in.shape[0]
    n_b = dil_w_qkv.shape[0]
    m = batch * seq
    hidden = ffn_w_gate.shape[-1]
    hidden_pad = -(-hidden // FFN_PAD_MULTIPLE) * FFN_PAD_MULTIPLE

    w_in = _pad_last(mla_w_in.astype(BF16), MLA_IN_PAD)
    qk_dim = MLA_NOPE_DIM + MLA_ROPE_DIM
    w_q_b = mla_w_q_b.astype(BF16).reshape(n_a, MLA_Q_RANK, MLA_HEADS, qk_dim)
    w_q_b = jnp.concatenate(
        [w_q_b[..., :MLA_NOPE_DIM].reshape(n_a, MLA_Q_RANK, MLA_HEADS * MLA_NOPE_DIM),
         w_q_b[..., MLA_NOPE_DIM:].reshape(n_a, MLA_Q_RANK, MLA_HEADS * MLA_ROPE_DIM)], axis=-1)
    w_kv_b = mla_w_kv_b.astype(BF16)

    g_mix = g_mix_norm[:, None, :]
    g_ffn = g_ffn_norm[:, None, :]
    g_q_a = mla_g_q_a[:, None, :]
    g_kv_a = mla_g_kv_a[:, None, :]
    g_q_nope = mla_g_q_nope[:, None, :]
    g_k_nope = mla_g_k_nope[:, None, :]
    g_q_pe = jnp.concatenate([mla_g_q_pe, mla_g_q_pe], axis=-1)[:, None, :]
    g_k_pe = _pad_last(mla_g_k_pe, LANES)[:, None, :]
    n_groups = len(DIL_GROUPS)
    n_qk = 2 * n_groups * DIL_HEADS * DIL_HEAD_DIM
    dil_gain = jnp.stack([dil_g_q, dil_g_k], axis=1)
    dil_gain = jnp.broadcast_to(dil_gain[:, :, :, None, :], (n_b, 2, n_groups, DIL_HEADS, DIL_HEAD_DIM))
    dil_gain = dil_gain.reshape(n_b, 1, n_qk)

    mod = _cond_mod(c, w_cond, b_cond, w_mod, b_mod)
    cos_m, s1_m, s2_m, cos_d, sin_d = _rope_tables(positions)

    xr = x.reshape(m, d)
    for i in range(depth):
        j = i // 2
        h = _norm_mod(xr, g_mix, mod, i, 0, 1, batch, seq)
        if i % 2 == 0:
            a = _matmul(h, w_in, j, tm=1024, tn=MLA_KV_BLOCK, out_dtype=F32)
            q_nope, q_pe = _mla_q_proj(a, g_q_a, w_q_b, j, g_q_nope, g_q_pe, (cos_m, s1_m, s2_m),
                                       tm=1024, tn=2048)
            kv, kpe = _mla_kv_proj(a, g_kv_a, w_kv_b, j, g_k_nope, g_k_pe, (cos_m, s1_m, s2_m),
                                   tm=1024, tn=2048)
            o, w_o = _mla_attention(q_nope, q_pe, kv, kpe, mla_w_o, j, batch, seq)
            xr = _matmul_resid(o, w_o[None], 0, xr, mod, i, 2, batch, seq, tm=1024, tn=512)
        else:
            qkv = _dil_qkv_proj(h, dil_w_qkv, j, dil_gain, (cos_d, sin_d), tm=2048, tn=512)
            o, w_o = _dil_attention(qkv, dil_w_o, j, batch, seq)
            xr = _matmul_resid(o, w_o[None], 0, xr, mod, i, 2, batch, seq, tm=1024, tn=1024)
        h = _norm_mod(xr, g_ffn, mod, i, 3, 4, batch, seq)
        act, w_down = _swiglu_up(h, ffn_w_gate, ffn_w_up, ffn_w_down, i, hidden_pad,
                                 tm=2048, tn=FFN_COL_BLOCK)
        xr = _matmul_resid(act, w_down[None], 0, xr, mod, i, 5, batch, seq, tm=1024, tn=1024,
                           tk=hidden_pad // 4)
    return xr.reshape(batch, seq, d)
```

```python
import functools
import math

import jax
import jax.numpy as jnp
from jax import lax
from jax.experimental import pallas as pl
from jax.experimental.pallas import tpu as pltpu

F32 = jnp.float32
BF16 = jnp.bfloat16

ROPE_THETA = 10000.0
NORM_EPS = 1e-6
N_MOD = 6

MLA_HEADS = 32
MLA_Q_RANK = 1536
MLA_KV_RANK = 512
MLA_NOPE_DIM = 128
MLA_ROPE_DIM = 64
MLA_V_DIM = 128

DIL_GROUPS = ((128, 1), (512, 4), (2048, 16))
DIL_HEADS = 16
DIL_HEAD_DIM = 128

LANES = 128
V7X_VMEM_LIMIT_BYTES = 56 * 2**20

FFN_PAD_MULTIPLE = 1024
EPILOGUE_ROWS = 256
FFN_COL_BLOCK = 256
MLA_IN_PAD = 2304
LOG2_E = math.log2(math.e)


def _params(*semantics):
    return pltpu.CompilerParams(dimension_semantics=semantics,
                                vmem_limit_bytes=V7X_VMEM_LIMIT_BYTES)


def _dot(a, b):
    return jnp.dot(a, b, preferred_element_type=F32)


def _dot_nt(a, b):
    return lax.dot_general(a, b, (((1,), (1,)), ((), ())), preferred_element_type=F32)


def _rms(x, gain, inv_n=None):
    if inv_n is None:
        ms = jnp.mean(x * x, axis=-1, keepdims=True)
    else:
        ms = jnp.sum(x * x, axis=-1, keepdims=True) * inv_n
    return x * lax.rsqrt(ms + NORM_EPS) * gain


def _cond_mod_kernel(c_ref, wc_ref, bc_ref, wm_ref, bm_ref, o_ref, e_ref):
    @pl.when((pl.program_id(0) == 0) & (pl.program_id(1) == 0))
    def _():
        z = _dot(c_ref[...].astype(BF16), wc_ref[...].astype(BF16)) + bc_ref[...]
        e_ref[...] = z * jax.nn.sigmoid(z)

    o_ref[...] = _dot(e_ref[...].astype(BF16), wm_ref[...].astype(BF16)) + bm_ref[...]


def _cond_mod(c, w_cond, b_cond, w_mod, b_mod):
    batch, d = c.shape
    depth, rank, n = w_mod.shape
    rows = 8
    tn = 2048
    c8 = jnp.pad(c, ((0, rows - batch), (0, 0)))
    out = pl.pallas_call(
        _cond_mod_kernel,
        grid=(depth, n // tn),
        in_specs=[
            pl.BlockSpec((rows, d), lambda l, j: (0, 0)),
            pl.BlockSpec((d, rank), lambda l, j: (0, 0)),
            pl.BlockSpec((1, rank), lambda l, j: (0, 0)),
            pl.BlockSpec((None, rank, tn), lambda l, j: (l, 0, j)),
            pl.BlockSpec((None, 1, tn), lambda l, j: (l, 0, j)),
        ],
        out_specs=pl.BlockSpec((None, rows, tn), lambda l, j: (l, 0, j)),
        out_shape=jax.ShapeDtypeStruct((depth, rows, n), F32),
        scratch_shapes=[pltpu.VMEM((rows, rank), F32)],
        compiler_params=_params("arbitrary", "arbitrary"),
        name="cond_mod",
    )(c8, w_cond, b_cond.reshape(1, rank), w_mod, b_mod.reshape(depth, 1, n))
    return out[:, :batch].reshape(depth * batch, 1, n)


def _rope_tables_kernel(pos_ref, cm_ref, s1_ref, s2_ref, cd_ref, sd_ref):
    pos = pos_ref[...].astype(F32)
    lane = lax.broadcasted_iota(jnp.int32, (1, LANES), 1)

    def angles(dim):
        half = dim // 2
        idx = (lane & (half - 1)).astype(F32)
        inv_freq = jnp.exp(idx * (-2.0 * math.log(ROPE_THETA) / dim))
        return pos * inv_freq

    ang = angles(MLA_ROPE_DIM)
    cos, sin = jnp.cos(ang), jnp.sin(ang)
    first_half = (lane & (MLA_ROPE_DIM - 1)) < MLA_ROPE_DIM // 2
    cm_ref[...] = cos
    s1_ref[...] = jnp.where(first_half, -sin, 0.0)
    s2_ref[...] = jnp.where(first_half, 0.0, sin)
    ang = angles(DIL_HEAD_DIM)
    cos, sin = jnp.cos(ang), jnp.sin(ang)
    cd_ref[...] = cos
    sd_ref[...] = jnp.where(lane < DIL_HEAD_DIM // 2, -sin, sin)


def _rope_tables(positions):
    batch, seq = positions.shape
    ts = 512
    m = batch * seq
    table = jax.ShapeDtypeStruct((m, LANES), F32)
    spec = pl.BlockSpec((ts, LANES), lambda i: (i, 0))
    return pl.pallas_call(
        _rope_tables_kernel,
        grid=(m // ts,),
        in_specs=[pl.BlockSpec((ts, 1), lambda i: (i, 0))],
        out_specs=[spec] * 5,
        out_shape=[table] * 5,
        compiler_params=_params("parallel"),
        name="rope_tables",
    )(positions.reshape(m, 1))


def _norm_mod_kernel(x_ref, g_ref, sc_ref, sh_ref, o_ref):
    y = _rms(x_ref[...], g_ref[...])
    o_ref[...] = (y * (1.0 + sc_ref[...]) + sh_ref[...]).astype(o_ref.dtype)


def _norm_mod(x, gain, mod, layer, shift_idx, scale_idx, batch, seq):
    m, d = x.shape
    tm = 512
    per_batch = seq // tm

    def mod_spec(which):
        return pl.BlockSpec((None, 1, d), lambda i: (layer * batch + i // per_batch, 0, which))

    return pl.pallas_call(
        _norm_mod_kernel,
        grid=(m // tm,),
        in_specs=[
            pl.BlockSpec((tm, d), lambda i: (i, 0)),
            pl.BlockSpec((None, 1, d), lambda i: (layer, 0, 0)),
            mod_spec(scale_idx),
            mod_spec(shift_idx),
        ],
        out_specs=pl.BlockSpec((tm, d), lambda i: (i, 0)),
        out_shape=jax.ShapeDtypeStruct((m, d), BF16),
        compiler_params=_params("parallel"),
        name="norm_mod",
    )(x, gain, mod, mod)


def _mm_kernel(x_ref, w_ref, o_ref):
    o_ref[...] = _dot(x_ref[...], w_ref[...].astype(BF16)).astype(o_ref.dtype)


def _resident_row_block(tm, k, index_map):
    return pl.BlockSpec((tm, k), index_map, pipeline_mode=pl.Buffered(1))


def _matmul(x, w, layer, *, tm, tn, out_dtype):
    m, k = x.shape
    n = w.shape[-1]
    return pl.pallas_call(
        _mm_kernel,
        grid=(m // tm, n // tn),
        in_specs=[
            pl.BlockSpec((tm, k), lambda i, j: (i, 0)),
            pl.BlockSpec((None, k, tn), lambda i, j: (layer, 0, j)),
        ],
        out_specs=pl.BlockSpec((tm, tn), lambda i, j: (i, j)),
        out_shape=jax.ShapeDtypeStruct((m, n), out_dtype),
        compiler_params=_params("parallel", "arbitrary"),
        name="matmul",
    )(x, w)


def _mm_resid_kernel(x_ref, w_ref, r_ref, gt_ref, o_ref):
    o_ref[...] = r_ref[...] + gt_ref[...] * _dot(x_ref[...], w_ref[...])


def _mm_resid_ksplit_kernel(x_ref, w_ref, r_ref, gt_ref, o_ref):
    @pl.when(pl.program_id(2) == 0)
    def _():
        o_ref[...] = r_ref[...] + gt_ref[...] * _dot(x_ref[...], w_ref[...])

    @pl.when(pl.program_id(2) > 0)
    def _():
        o_ref[...] += gt_ref[...] * _dot(x_ref[...], w_ref[...])


def _matmul_resid(x, w, layer, resid, mod, mod_layer, gate_idx, batch, seq, *, tm, tn, tk=None):
    m, k = x.shape
    n = w.shape[-1]
    per_batch = seq // tm
    gate_blocks = n // tn
    if tk is None:
        return pl.pallas_call(
            _mm_resid_kernel,
            grid=(m // tm, n // tn),
            in_specs=[
                pl.BlockSpec((tm, k), lambda i, j: (i, 0)),
                pl.BlockSpec((None, k, tn), lambda i, j: (layer, 0, j)),
                pl.BlockSpec((tm, tn), lambda i, j: (i, j)),
                pl.BlockSpec((None, 1, tn), lambda i, j: (mod_layer * batch + i // per_batch, 0,
                                                          gate_idx * gate_blocks + j)),
            ],
            out_specs=pl.BlockSpec((tm, tn), lambda i, j: (i, j)),
            out_shape=jax.ShapeDtypeStruct((m, n), F32),
            compiler_params=_params("parallel", "arbitrary"),
            name="matmul_resid",
        )(x, w, resid, mod)
    return pl.pallas_call(
        _mm_resid_ksplit_kernel,
        grid=(m // tm, n // tn, k // tk),
        in_specs=[
            pl.BlockSpec((tm, tk), lambda i, j, kk: (i, kk)),
            pl.BlockSpec((None, tk, tn), lambda i, j, kk: (layer, kk, j)),
            pl.BlockSpec((tm, tn), lambda i, j, kk: (i, j)),
            pl.BlockSpec((None, 1, tn), lambda i, j, kk: (mod_layer * batch + i // per_batch, 0,
                                                          gate_idx * gate_blocks + j)),
        ],
        out_specs=pl.BlockSpec((tm, tn), lambda i, j, kk: (i, j)),
        out_shape=jax.ShapeDtypeStruct((m, n), F32),
        compiler_params=_params("parallel", "parallel", "arbitrary"),
        name="matmul_resid_ksplit",
    )(x, w, resid, mod)


def _swiglu_kernel(n_src_blocks, x_ref, wg_ref, wu_ref, wd_ref, o_ref, wd_out_ref):
    first_sweep = pl.program_id(0) == 0
    in_range = pl.program_id(1) < n_src_blocks

    @pl.when(in_range)
    def _():
        x = x_ref[...]
        g = _dot(x, wg_ref[...].astype(BF16))
        u = _dot(x, wu_ref[...].astype(BF16))
        o_ref[...] = (g * jax.nn.sigmoid(g) * u).astype(o_ref.dtype)

    @pl.when(jnp.logical_not(in_range))
    def _():
        o_ref[...] = jnp.zeros_like(o_ref)

    @pl.when(first_sweep & in_range)
    def _():
        wd_out_ref[...] = wd_ref[...].astype(wd_out_ref.dtype)

    @pl.when(first_sweep & jnp.logical_not(in_range))
    def _():
        wd_out_ref[...] = jnp.zeros_like(wd_out_ref)


def _swiglu_up(x, wg, wu, wd, layer, n_out, *, tm, tn):
    m, k = x.shape
    d = wd.shape[-1]
    n_src_blocks = wg.shape[-1] // tn
    n_dst_blocks = n_out // tn
    w_spec = pl.BlockSpec((None, k, tn), lambda i, j: (layer, 0, jnp.minimum(j, n_src_blocks - 1)))
    wd_in = pl.BlockSpec((None, tn, d), lambda i, j: (
        layer, jnp.where(i == 0, jnp.minimum(j, n_src_blocks - 1), n_src_blocks - 1), 0))
    wd_out = pl.BlockSpec((tn, d), lambda i, j: (jnp.where(i == 0, j, n_dst_blocks - 1), 0))
    return pl.pallas_call(
        functools.partial(_swiglu_kernel, n_src_blocks),
        grid=(m // tm, n_dst_blocks),
        in_specs=[_resident_row_block(tm, k, lambda i, j: (i, 0)), w_spec, w_spec, wd_in],
        out_specs=[pl.BlockSpec((tm, tn), lambda i, j: (i, j)), wd_out],
        out_shape=[jax.ShapeDtypeStruct((m, n_out), BF16), jax.ShapeDtypeStruct((n_out, d), BF16)],
        compiler_params=_params("arbitrary", "arbitrary"),
        name="swiglu_up",
    )(x, wg, wu, wd)


def _rope_mla(x, cos, s1, s2):
    half = MLA_ROPE_DIM // 2
    return (x * cos + pltpu.roll(x, LANES - half, 1) * s1 + pltpu.roll(x, half, 1) * s2)


def _rope_dil(x, cos, sin_signed):
    return x * cos + pltpu.roll(x, DIL_HEAD_DIM // 2, 1) * sin_signed


def _deferred_schedule(n_i, n_j):
    steps = n_i * n_j

    def cur(t):
        t = jnp.minimum(t, steps - 1)
        return t // n_j, t % n_j

    def prev(t):
        t = jnp.maximum(t - 1, 0)
        return t // n_j, t % n_j

    return steps, cur, prev


def _deferred_prologue(n_j, acc_ref):
    t = pl.program_id(0)

    @pl.when(t == 0)
    def _():
        acc_ref[...] = jnp.zeros_like(acc_ref)

    return lax.rem(t, n_j) == 0


def _mla_q_nope_kernel(n_j, cq_ref, gqa_ref, w_ref, gn_ref, o_ref, hs_ref, acc_ref):
    @pl.when(_deferred_prologue(n_j, acc_ref))
    def _():
        hs_ref[...] = _rms(cq_ref[...], gqa_ref[...]).astype(hs_ref.dtype)

    for t in range(acc_ref.shape[1] // LANES):
        cols = slice(t * LANES, (t + 1) * LANES)
        o_ref[:, cols] = _rms(acc_ref[:, cols], gn_ref[...]).astype(o_ref.dtype)
    acc_ref[...] = _dot(hs_ref[...], w_ref[...])


def _mla_q_pe_kernel(n_j, cq_ref, gqa_ref, w_ref, gp_ref, cos_ref, s1_ref, s2_ref, o_ref, hs_ref, acc_ref):
    @pl.when(_deferred_prologue(n_j, acc_ref))
    def _():
        hs_ref[...] = _rms(cq_ref[...], gqa_ref[...]).astype(hs_ref.dtype)

    cos, s1, s2 = cos_ref[...], s1_ref[...], s2_ref[...]
    low = lax.broadcasted_iota(jnp.int32, (1, LANES), 1) < MLA_ROPE_DIM
    for t in range(acc_ref.shape[1] // LANES):
        cols = slice(t * LANES, (t + 1) * LANES)
        x = acc_ref[:, cols]
        sq = x * x
        ss_low = jnp.sum(jnp.where(low, sq, 0.0), axis=-1, keepdims=True)
        ss_high = jnp.sum(jnp.where(low, 0.0, sq), axis=-1, keepdims=True)
        ms = jnp.where(low, ss_low, ss_high) * (1.0 / MLA_ROPE_DIM)
        y = x * lax.rsqrt(ms + NORM_EPS) * gp_ref[...]
        o_ref[:, cols] = _rope_mla(y, cos, s1, s2).astype(o_ref.dtype)
    acc_ref[...] = _dot(hs_ref[...], w_ref[...])


def _mla_q_proj(a, g_q_a, w, layer, g_nope, g_pe_pair, tables, *, tm, tn):
    m = a.shape[0]
    k = w.shape[1]
    n_nope = MLA_HEADS * MLA_NOPE_DIM
    n_pe = MLA_HEADS * MLA_ROPE_DIM
    cos, s1, s2 = tables
    gain = pl.BlockSpec((None, 1, LANES), lambda t: (layer, 0, 0))

    def call(body, name, n_out, col_block0, with_tables, extra_args):
        n_j = n_out // tn
        steps, cur, prev = _deferred_schedule(m // tm, n_j)
        tab = pl.BlockSpec((tm, LANES), lambda t: (prev(t)[0], 0))
        return pl.pallas_call(
            functools.partial(body, n_j),
            grid=(steps + 1,),
            in_specs=[
                pl.BlockSpec((tm, k), lambda t: (cur(t)[0], 0)),
                pl.BlockSpec((None, 1, k), lambda t: (layer, 0, 0)),
                pl.BlockSpec((None, k, tn), lambda t: (layer, 0, col_block0 + cur(t)[1])),
                gain,
            ] + ([tab, tab, tab] if with_tables else []),
            out_specs=pl.BlockSpec((tm, tn), lambda t: prev(t)),
            out_shape=jax.ShapeDtypeStruct((m, n_out), BF16),
            scratch_shapes=[pltpu.VMEM((tm, k), BF16), pltpu.VMEM((tm, tn), F32)],
            compiler_params=_params("arbitrary"),
            name=name,
        )(a, g_q_a, w, *extra_args)

    q_nope = call(_mla_q_nope_kernel, "mla_q_nope_proj", n_nope, 0, False, [g_nope])
    q_pe = call(_mla_q_pe_kernel, "mla_q_pe_proj", n_pe, n_nope // tn, True, [g_pe_pair, cos, s1, s2])
    return q_nope, q_pe


MLA_KV_BLOCK = 768


def _mla_kv_kernel(ckv_ref, gkva_ref, w_ref, gk_ref, gp_ref, cos_ref, s1_ref, s2_ref,
                   kv_ref, kpe_ref, hs_ref):
    @pl.when(pl.program_id(1) == 0)
    def _():
        blk = ckv_ref[...]
        hs_ref[...] = _rms(blk[:, :MLA_KV_RANK], gkva_ref[...]).astype(hs_ref.dtype)
        pe = _rms(blk[:, MLA_KV_RANK:MLA_KV_RANK + LANES], gp_ref[...], inv_n=1.0 / MLA_ROPE_DIM)
        pe = _rope_mla(pe, cos_ref[...], s1_ref[...], s2_ref[...])
        kpe_ref[:, :LANES] = pe.astype(kpe_ref.dtype)
        kpe_ref[:, LANES:] = pltpu.roll(pe, MLA_ROPE_DIM, 1).astype(kpe_ref.dtype)

    acc = _dot(hs_ref[...], w_ref[...])
    head_w = MLA_NOPE_DIM + MLA_V_DIM
    for hh in range(acc.shape[1] // head_w):
        k_cols = slice(hh * head_w, hh * head_w + MLA_NOPE_DIM)
        v_cols = slice(hh * head_w + MLA_NOPE_DIM, (hh + 1) * head_w)
        kv_ref[:, k_cols] = _rms(acc[:, k_cols], gk_ref[...]).astype(kv_ref.dtype)
        kv_ref[:, v_cols] = acc[:, v_cols].astype(kv_ref.dtype)


def _mla_kv_proj(a, g_kv_a, w, layer, g_k_nope, g_pe_pad, tables, *, tm, tn):
    m = a.shape[0]
    k, n = w.shape[1:]
    cos, s1, s2 = tables
    tab = pl.BlockSpec((tm, LANES), lambda i, j: (i, 0))
    gain = pl.BlockSpec((None, 1, LANES), lambda i, j: (layer, 0, 0))
    kv_col_block = MLA_Q_RANK // MLA_KV_BLOCK
    return pl.pallas_call(
        _mla_kv_kernel,
        grid=(m // tm, n // tn),
        in_specs=[
            pl.BlockSpec((tm, MLA_KV_BLOCK), lambda i, j: (i, kv_col_block)),
            pl.BlockSpec((None, 1, k), lambda i, j: (layer, 0, 0)),
            pl.BlockSpec((None, k, tn), lambda i, j: (layer, 0, j)),
            gain, gain,
            tab, tab, tab,
        ],
        out_specs=[
            pl.BlockSpec((tm, tn), lambda i, j: (i, j)),
            pl.BlockSpec((tm, 2 * LANES), lambda i, j: (i, 0)),
        ],
        out_shape=[
            jax.ShapeDtypeStruct((m, n), BF16),
            jax.ShapeDtypeStruct((m, 2 * LANES), BF16),
        ],
        scratch_shapes=[pltpu.VMEM((tm, k), BF16)],
        compiler_params=_params("parallel", "arbitrary"),
        name="mla_kv_proj",
    )(a, g_kv_a, w, g_k_nope, g_pe_pad, cos, s1, s2)


def _softmax_pv(scores, values, scale):
    c = scale * LOG2_E
    m = None
    for s in scores:
        mi = jnp.max(s, axis=-1, keepdims=True)
        m = mi if m is None else jnp.maximum(m, mi)
    mc = m * c
    den = None
    acc = None
    for s, v in zip(scores, values):
        p = jnp.exp2(s * c - mc)
        li = jnp.sum(p, axis=-1, keepdims=True)
        ai = _dot(p.astype(v.dtype), v)
        den = li if den is None else den + li
        acc = ai if acc is None else acc + ai
    return acc / den


MLA_Q_BLOCK = 512


def _mla_attn_kernel(qn_ref, qpe_ref, kv_ref, kpe_ref, wo_ref, o_ref, wo_out_ref, kcat_ref):
    wo_out_ref[...] = wo_ref[...].astype(wo_out_ref.dtype)
    seq = qn_ref.shape[0]
    tq = MLA_Q_BLOCK
    scale = (MLA_NOPE_DIM + MLA_ROPE_DIM) ** -0.5
    head_w = MLA_NOPE_DIM + MLA_V_DIM
    for hh in range(2):
        kcat_ref[hh, :, :MLA_NOPE_DIM] = kv_ref[:, hh * head_w:hh * head_w + MLA_NOPE_DIM]
        kcat_ref[hh, :, MLA_NOPE_DIM:] = kpe_ref[:, hh * LANES:(hh + 1) * LANES]

    row = lax.broadcasted_iota(jnp.int32, (tq, tq), 0)
    col = lax.broadcasted_iota(jnp.int32, (tq, tq), 1)
    causal = col <= row
    def block_scores(qb, hh):
        lo = qb * tq
        q = jnp.concatenate([qn_ref[lo:lo + tq, hh * LANES:(hh + 1) * LANES], qpe_ref[lo:lo + tq, :]], axis=1)
        v0 = hh * head_w + MLA_NOPE_DIM
        scores = [jnp.where(causal, _dot_nt(q, kcat_ref[hh, lo:lo + tq, :]), -jnp.inf)]
        values = [kv_ref[lo:lo + tq, v0:v0 + MLA_V_DIM]]
        if qb > 0:
            scores.append(_dot_nt(q, kcat_ref[hh, 0:lo, :]))
            values.append(kv_ref[0:lo, v0:v0 + MLA_V_DIM])
        return scores, values

    order = [(qb, hh) for qb in range(seq // tq) for hh in range(2)]
    pending = block_scores(*order[0])
    for idx, (qb, hh) in enumerate(order):
        upcoming = block_scores(*order[idx + 1]) if idx + 1 < len(order) else None
        o = _softmax_pv(*pending, scale)
        o_ref[qb * tq:(qb + 1) * tq, hh * MLA_V_DIM:(hh + 1) * MLA_V_DIM] = o.astype(o_ref.dtype)
        pending = upcoming


def _mla_attention(q_nope, q_pe, kv, kpe, w_o, layer, batch, seq):
    pairs = MLA_HEADS // 2
    wo_rows, wo_cols = w_o.shape[1:]
    slab = wo_rows // (batch * pairs)
    assert slab * batch * pairs == wo_rows
    return pl.pallas_call(
        _mla_attn_kernel,
        grid=(batch, pairs),
        in_specs=[
            pl.BlockSpec((seq, 2 * MLA_NOPE_DIM), lambda b, p: (b, p)),
            pl.BlockSpec((seq, LANES), lambda b, p: (b, p)),
            pl.BlockSpec((seq, 2 * (MLA_NOPE_DIM + MLA_V_DIM)), lambda b, p: (b, p)),
            pl.BlockSpec((seq, 2 * LANES), lambda b, p: (b, 0)),
            pl.BlockSpec((None, slab, wo_cols), lambda b, p: (layer, b * pairs + p, 0)),
        ],
        out_specs=[
            pl.BlockSpec((seq, 2 * MLA_V_DIM), lambda b, p: (b, p)),
            pl.BlockSpec((slab, wo_cols), lambda b, p: (b * pairs + p, 0)),
        ],
        out_shape=[
            jax.ShapeDtypeStruct((batch * seq, MLA_HEADS * MLA_V_DIM), BF16),
            jax.ShapeDtypeStruct((wo_rows, wo_cols), BF16),
        ],
        scratch_shapes=[pltpu.VMEM((2, seq, MLA_NOPE_DIM + LANES), BF16)],
        compiler_params=_params("parallel", "parallel"),
        name="mla_attention",
    )(q_nope, q_pe, kv, kpe, w_o)


def _dil_qkv_kernel(n_j, n_qk_blocks, x_ref, w_ref, gain_ref, cos_ref, sin_ref, o_ref, acc_ref):
    _deferred_prologue(n_j, acc_ref)
    finished_col = lax.rem(jnp.maximum(pl.program_id(0) - 1, 0), n_j)
    is_v = finished_col >= n_qk_blocks
    for r0 in range(0, acc_ref.shape[0], EPILOGUE_ROWS):
        rows = slice(r0, r0 + EPILOGUE_ROWS)
        cos, sin = cos_ref[rows, :], sin_ref[rows, :]
        for hh in range(acc_ref.shape[1] // DIL_HEAD_DIM):
            cols = slice(hh * DIL_HEAD_DIM, (hh + 1) * DIL_HEAD_DIM)
            x = acc_ref[rows, cols]
            y = _rope_dil(_rms(x, gain_ref[:, cols]), cos, sin)
            o_ref[rows, cols] = jnp.where(is_v, x, y).astype(o_ref.dtype)
    acc_ref[...] = _dot(x_ref[...], w_ref[...].astype(BF16))


def _dil_qkv_proj(x, w, layer, gain_cols, tables, *, tm, tn):
    m, k = x.shape
    n = w.shape[-1]
    cos, sin = tables
    n_j = n // tn
    n_qk_blocks = gain_cols.shape[-1] // tn
    steps, cur, prev = _deferred_schedule(m // tm, n_j)
    tab = pl.BlockSpec((tm, LANES), lambda t: (prev(t)[0], 0))
    return pl.pallas_call(
        functools.partial(_dil_qkv_kernel, n_j, n_qk_blocks),
        grid=(steps + 1,),
        in_specs=[
            _resident_row_block(tm, k, lambda t: (cur(t)[0], 0)),
            pl.BlockSpec((None, k, tn), lambda t: (layer, 0, cur(t)[1])),
            pl.BlockSpec((None, 1, tn), lambda t: (layer, 0, jnp.minimum(prev(t)[1], n_qk_blocks - 1))),
            tab, tab,
        ],
        out_specs=pl.BlockSpec((tm, tn), lambda t: prev(t)),
        out_shape=jax.ShapeDtypeStruct((m, n), BF16),
        scratch_shapes=[pltpu.VMEM((tm, tn), F32)],
        compiler_params=_params("arbitrary"),
        name="dil_qkv_proj",
    )(x, w, gain_cols, cos, sin)


DIL_Q_BLOCK = 128


def _dil_bias(rows, cols, offset, window, dilation):
    dist = (offset + lax.broadcasted_iota(jnp.int32, (rows, cols), 0)
            - lax.broadcasted_iota(jnp.int32, (rows, cols), 1))
    valid = (dist >= 0) & (dist <= window) & ((dist & (dilation - 1)) == 0)
    return jnp.where(valid, 0.0, -jnp.inf).astype(F32)


def _dil_attn_kernel(*refs):
    n_groups = len(DIL_GROUPS)
    q_refs = refs[:n_groups]
    k_refs = refs[n_groups:2 * n_groups]
    v_refs = refs[2 * n_groups:3 * n_groups]
    wo_ref, o_ref, wo_out_ref = refs[3 * n_groups:]
    wo_out_ref[...] = wo_ref[...].astype(wo_out_ref.dtype)
    seq = o_ref.shape[0]
    tq = DIL_Q_BLOCK
    scale = DIL_HEAD_DIM ** -0.5
    diag, full, cut = [], [], []
    for window, dilation in DIL_GROUPS:
        assert tq % dilation == 0 and (window % tq == 0 or tq % window == 0)
        diag.append(_dil_bias(tq, tq, 0, window, dilation))
        full.append(_dil_bias(tq, window, window, window, dilation) if window < seq else None)
        n_cut = min(window, seq) - tq
        cut.append(_dil_bias(tq, n_cut, n_cut, seq, dilation) if n_cut > 0 and dilation > 1 else None)

    def block_scores(t0):
        scores, values = [], []
        for g, (window, dilation) in enumerate(DIL_GROUPS):
            q = q_refs[g][t0:t0 + tq, :]
            scores.append(_dot_nt(q, k_refs[g][t0:t0 + tq, :]) + diag[g])
            values.append(v_refs[g][t0:t0 + tq, :])
            n_prev = min(window, t0)
            if n_prev == 0:
                continue
            s_prev = _dot_nt(q, k_refs[g][t0 - n_prev:t0, :])
            if n_prev == window:
                s_prev = s_prev + full[g]
            elif cut[g] is not None:
                s_prev = s_prev + cut[g][:, cut[g].shape[1] - n_prev:]
            scores.append(s_prev)
            values.append(v_refs[g][t0 - n_prev:t0, :])
        return scores, values

    n_blocks = seq // tq
    pending = block_scores(0)
    for sb in range(n_blocks):
        upcoming = block_scores((sb + 1) * tq) if sb + 1 < n_blocks else None
        o = _softmax_pv(*pending, scale)
        o_ref[sb * tq:(sb + 1) * tq, :] = o.astype(o_ref.dtype)
        pending = upcoming


def _dil_attention(qkv, w_o, layer, batch, seq):
    n_groups = len(DIL_GROUPS)
    hd = DIL_HEAD_DIM
    wo_rows, wo_cols = w_o.shape[1:]
    slab = wo_rows // (batch * DIL_HEADS)
    assert slab * batch * DIL_HEADS == wo_rows

    def spec(part, g):
        c = (part * n_groups + g) * DIL_HEADS
        return pl.BlockSpec((seq, hd), lambda b, h: (b, c + h))

    return pl.pallas_call(
        _dil_attn_kernel,
        grid=(batch, DIL_HEADS),
        in_specs=[spec(part, g) for part in range(3) for g in range(n_groups)]
        + [pl.BlockSpec((None, slab, wo_cols), lambda b, h: (layer, b * DIL_HEADS + h, 0))],
        out_specs=[
            pl.BlockSpec((seq, hd), lambda b, h: (b, h)),
            pl.BlockSpec((slab, wo_cols), lambda b, h: (b * DIL_HEADS + h, 0)),
        ],
        out_shape=[
            jax.ShapeDtypeStruct((batch * seq, DIL_HEADS * hd), BF16),
            jax.ShapeDtypeStruct((wo_rows, wo_cols), BF16),
        ],
        compiler_params=_params("parallel", "parallel"),
        name="dil_attention",
    )(*([qkv] * (3 * n_groups) + [w_o]))


def _pad_last(x, width):
    return jnp.pad(x, [(0, 0)] * (x.ndim - 1) + [(0, width - x.shape[-1])])


def kernel(x, c, positions, w_cond, b_cond, w_mod, b_mod, g_mix_norm, g_ffn_norm, mla_w_in, mla_g_q_a, mla_g_kv_a, mla_w_q_b, mla_w_kv_b, mla_g_q_nope, mla_g_q_pe, mla_g_k_nope, mla_g_k_pe, mla_w_o, dil_w_qkv, dil_g_q, dil_g_k, dil_w_o, ffn_w_gate, ffn_w_up, ffn_w_down):
    batch, seq, d = x.shape
    depth = w_mod.shape[0]
    n_a = mla_w_in.shape[0]
    n_b = dil_w_qkv.shape[0]
    m = batch * seq
    hidden = ffn_w_gate.shape[-1]
    hidden_pad = -(-hidden // FFN_PAD_MULTIPLE) * FFN_PAD_MULTIPLE

    w_in = _pad_last(mla_w_in.astype(BF16), MLA_IN_PAD)
    qk_dim = MLA_NOPE_DIM + MLA_ROPE_DIM
    w_q_b = mla_w_q_b.astype(BF16).reshape(n_a, MLA_Q_RANK, MLA_HEADS, qk_dim)
    w_q_b = jnp.concatenate(
        [w_q_b[..., :MLA_NOPE_DIM].reshape(n_a, MLA_Q_RANK, MLA_HEADS * MLA_NOPE_DIM),
         w_q_b[..., MLA_NOPE_DIM:].reshape(n_a, MLA_Q_RANK, MLA_HEADS * MLA_ROPE_DIM)], axis=-1)
    w_kv_b = mla_w_kv_b.astype(BF16)

    g_mix = g_mix_norm[:, None, :]
    g_ffn = g_ffn_norm[:, None, :]
    g_q_a = mla_g_q_a[:, None, :]
    g_kv_a = mla_g_kv_a[:, None, :]
    g_q_nope = mla_g_q_nope[:, None, :]
    g_k_nope = mla_g_k_nope[:, None, :]
    g_q_pe = jnp.concatenate([mla_g_q_pe, mla_g_q_pe], axis=-1)[:, None, :]
    g_k_pe = _pad_last(mla_g_k_pe, LANES)[:, None, :]
    n_groups = len(DIL_GROUPS)
    n_qk = 2 * n_groups * DIL_HEADS * DIL_HEAD_DIM
    dil_gain = jnp.stack([dil_g_q, dil_g_k], axis=1)
    dil_gain = jnp.broadcast_to(dil_gain[:, :, :, None, :], (n_b, 2, n_groups, DIL_HEADS, DIL_HEAD_DIM))
    dil_gain = dil_gain.reshape(n_b, 1, n_qk)

    mod = _cond_mod(c, w_cond, b_cond, w_mod, b_mod)
    cos_m, s1_m, s2_m, cos_d, sin_d = _rope_tables(positions)

    xr = x.reshape(m, d)
    for i in range(depth):
        j = i // 2
        h = _norm_mod(xr, g_mix, mod, i, 0, 1, batch, seq)
        if i % 2 == 0:
            a = _matmul(h, w_in, j, tm=1024, tn=MLA_KV_BLOCK, out_dtype=F32)
            q_nope, q_pe = _mla_q_proj(a, g_q_a, w_q_b, j, g_q_nope, g_q_pe, (cos_m, s1_m, s2_m),
                                       tm=1024, tn=2048)
            kv, kpe = _mla_kv_proj(a, g_kv_a, w_kv_b, j, g_k_nope, g_k_pe, (cos_m, s1_m, s2_m),
                                   tm=1024, tn=2048)
            o, w_o = _mla_attention(q_nope, q_pe, kv, kpe, mla_w_o, j, batch, seq)
            xr = _matmul_resid(o, w_o[None], 0, xr, mod, i, 2, batch, seq, tm=1024, tn=1024)
        else:
            qkv = _dil_qkv_proj(h, dil_w_qkv, j, dil_gain, (cos_d, sin_d), tm=2048, tn=512)
            o, w_o = _dil_attention(qkv, dil_w_o, j, batch, seq)
            xr = _matmul_resid(o, w_o[None], 0, xr, mod, i, 2, batch, seq, tm=1024, tn=1024)
        h = _norm_mod(xr, g_ffn, mod, i, 3, 4, batch, seq)
        act, w_down = _swiglu_up(h, ffn_w_gate, ffn_w_up, ffn_w_down, i, hidden_pad,
                                 tm=2048, tn=FFN_COL_BLOCK)
        xr = _matmul_resid(act, w_down[None], 0, xr, mod, i, 5, batch, seq, tm=1024, tn=1024,
                           tk=hidden_pad // 4)
    return xr.reshape(batch, seq, d)
```
